```python
import jax, jax.numpy as jnp
from jax import lax
import numpy as np

D_MODEL = 1024
BATCH = 8
SEQ = 8192
DEPTH = 1

CHUNK = 64
EPS = 1e-5
N_BRANCH = 2
GMLP_BLOCK = 128
GMLP_WIDTH = 1024
GMLP_GROUPS = 8
GMLP_GDIM = GMLP_WIDTH // GMLP_GROUPS
SSM_INNER = 2 * D_MODEL
SSM_HEAD_DIM = 64
SSM_HEADS = SSM_INNER // SSM_HEAD_DIM
SSM_GROUPS = 4
SSM_HPG = SSM_HEADS // SSM_GROUPS
SSM_STATE = 128
SSM_CONV = 4
SSM_CHUNK = CHUNK
SSM_XBC = SSM_INNER + 2 * SSM_GROUPS * SSM_STATE
D_FF = 2816
FFN_CONV = 3
IN_COLS = N_BRANCH * D_MODEL + 2 * GMLP_WIDTH + SSM_INNER + SSM_XBC + SSM_HEADS

kernel_name = "hybrid_gmlp_ssd_gated_merge_block"


def rmsnorm(x, w):
    xf = x.astype(jnp.float32)
    y = xf * lax.rsqrt(jnp.mean(xf * xf, axis=-1, keepdims=True) + EPS)
    return (y * w.astype(jnp.float32)).astype(x.dtype)


def causal_dwconv(x, w, b):
    K, C = w.shape
    y = lax.conv_general_dilated(
        x, w[:, None, :].astype(x.dtype), window_strides=(1,), padding=[(K - 1, 0)],
        dimension_numbers=('NWC', 'WIO', 'NWC'), feature_group_count=C)
    return y + b.astype(x.dtype)


def gmlp_mixer(za, ln_w, ln_b, w_s, b_s):
    Bsz, S, _ = za.shape
    z = jax.nn.gelu(za)
    u, v = jnp.split(z, 2, axis=-1)
    nb = S // GMLP_BLOCK
    v = v.reshape(Bsz, nb, GMLP_BLOCK, GMLP_GROUPS, GMLP_GDIM)
    vf = v.astype(jnp.float32)
    mu = jnp.mean(vf, axis=-1, keepdims=True)
    var = jnp.mean(jnp.square(vf - mu), axis=-1, keepdims=True)
    v = ((vf - mu) * lax.rsqrt(var + EPS) * ln_w.astype(jnp.float32)
         + ln_b.astype(jnp.float32)).astype(z.dtype)
    chunk_id = jnp.arange(GMLP_BLOCK) // CHUNK
    mask = chunk_id[None, :] <= chunk_id[:, None]
    ws = jnp.where(mask[None], w_s, 0).astype(v.dtype)
    sv = jnp.einsum('gts,bnsgd->bntgd', ws, v) + b_s.T.astype(v.dtype)[None, None, :, :, None]
    return u * sv.reshape(Bsz, S, GMLP_WIDTH)


def ssd_scan(x, dt, A, Bm, Cm):
    Bsz, S = x.shape[:2]
    L = SSM_CHUNK
    nc = S // L
    f32 = jnp.float32
    x = x.astype(f32).reshape(Bsz, nc, L, SSM_GROUPS, SSM_HPG, SSM_HEAD_DIM)
    dt = dt.reshape(Bsz, nc, L, SSM_GROUPS, SSM_HPG)
    Bm = Bm.astype(f32).reshape(Bsz, nc, L, SSM_GROUPS, SSM_STATE)
    Cm = Cm.astype(f32).reshape(Bsz, nc, L, SSM_GROUPS, SSM_STATE)
    a_cum = jnp.cumsum(dt * A.reshape(SSM_GROUPS, SSM_HPG), axis=2)
    xdt = x * dt[..., None]
    seg = a_cum[:, :, :, None] - a_cum[:, :, None, :]
    causal = jnp.tril(jnp.ones((L, L), dtype=bool))
    decay = jnp.exp(jnp.where(causal[:, :, None, None], seg, -jnp.inf))
    cb = jnp.einsum('bclgn,bcsgn->bclsg', Cm, Bm)
    y_diag = jnp.einsum('bclsg,bclsgh,bcsghp->bclghp', cb, decay, xdt)
    decay_end = jnp.exp(a_cum[:, :, -1:] - a_cum)
    states = jnp.einsum('bcsgn,bcsgh,bcsghp->bcghpn', Bm, decay_end, xdt)
    chunk_decay = jnp.exp(a_cum[:, :, -1])

    def step(h, inp):
        st, dec = inp
        return h * dec[..., None, None] + st, h

    h0 = jnp.zeros((Bsz, SSM_GROUPS, SSM_HPG, SSM_HEAD_DIM, SSM_STATE), f32)
    _, prev = lax.scan(step, h0, (jnp.moveaxis(states, 1, 0), jnp.moveaxis(chunk_decay, 1, 0)))
    prev = jnp.moveaxis(prev, 0, 1)
    y_off = jnp.einsum('bclgn,bcghpn,bclgh->bclghp', Cm, prev, jnp.exp(a_cum))
    return (y_diag + y_off).reshape(Bsz, S, SSM_HEADS, SSM_HEAD_DIM)


def mamba2_mixer(z, xbc, dt_raw, conv_w, conv_b, dt_bias, a_log, d_skip, norm_w):
    Bsz, S, _ = z.shape
    xbc = jax.nn.silu(causal_dwconv(xbc, conv_w, conv_b))
    xs, Bm, Cm = jnp.split(xbc, [SSM_INNER, SSM_INNER + SSM_GROUPS * SSM_STATE], axis=-1)
    xs = xs.reshape(Bsz, S, SSM_HEADS, SSM_HEAD_DIM)
    Bm = Bm.reshape(Bsz, S, SSM_GROUPS, SSM_STATE)
    Cm = Cm.reshape(Bsz, S, SSM_GROUPS, SSM_STATE)
    dt = jax.nn.softplus(dt_raw.astype(jnp.float32) + dt_bias.astype(jnp.float32))
    A = -jnp.exp(a_log.astype(jnp.float32))
    y = ssd_scan(xs, dt, A, Bm, Cm) + d_skip.astype(jnp.float32)[:, None] * xs.astype(jnp.float32)
    y = y.reshape(Bsz, S, SSM_INNER) * jax.nn.silu(z.astype(jnp.float32))
    yg = y.reshape(Bsz, S, SSM_GROUPS, SSM_INNER // SSM_GROUPS)
    yg = yg * lax.rsqrt(jnp.mean(yg * yg, axis=-1, keepdims=True) + EPS)
    y = yg.reshape(Bsz, S, SSM_INNER) * norm_w.astype(jnp.float32)
    return y.astype(z.dtype)


def setup_inputs(seed: int = 0) -> dict:
    key = jax.random.key(seed)
    ks = jax.random.split(key, 24)

    def nrm(k, shape, scale):
        return jax.random.normal(k, shape, jnp.float32) * scale

    def gain(k, shape):
        return 1.0 + 0.02 * jax.random.normal(k, shape, jnp.float32)

    dt0 = jnp.exp(jax.random.uniform(ks[12], (DEPTH, SSM_HEADS), jnp.float32,
                                     np.log(1e-3), np.log(1e-1)))
    dt_bias = dt0 + jnp.log(-jnp.expm1(-dt0))
    a_log = jnp.log(jax.random.uniform(ks[13], (DEPTH, SSM_HEADS), jnp.float32, 1.0, 16.0))
    return {
        "x": nrm(ks[0], (BATCH, SEQ, D_MODEL), 1.0),
        "mix_norm_w": gain(ks[1], (DEPTH, D_MODEL)),
        "w_in": nrm(ks[2], (DEPTH, D_MODEL, IN_COLS), D_MODEL ** -0.5),
        "gate_bias": nrm(ks[3], (DEPTH, N_BRANCH, D_MODEL), 0.01),
        "gmlp_ln_w": gain(ks[4], (DEPTH, GMLP_GROUPS, GMLP_GDIM)),
        "gmlp_ln_b": nrm(ks[5], (DEPTH, GMLP_GROUPS, GMLP_GDIM), 0.01),
        "gmlp_ws": nrm(ks[6], (DEPTH, GMLP_GROUPS, GMLP_BLOCK, GMLP_BLOCK), 0.5 * GMLP_BLOCK ** -0.5),
        "gmlp_bs": gain(ks[7], (DEPTH, GMLP_GROUPS, GMLP_BLOCK)),
        "ssm_conv_w": nrm(ks[8], (DEPTH, SSM_CONV, SSM_XBC), SSM_CONV ** -0.5),
        "ssm_conv_b": nrm(ks[9], (DEPTH, SSM_XBC), 0.01),
        "ssm_dt_bias": dt_bias,
        "ssm_a_log": a_log,
        "ssm_d": gain(ks[10], (DEPTH, SSM_HEADS)),
        "ssm_norm_w": gain(ks[11], (DEPTH, SSM_INNER)),
        "w_proj_a": nrm(ks[14], (DEPTH, GMLP_WIDTH, D_MODEL), GMLP_WIDTH ** -0.5),
        "w_proj_b": nrm(ks[15], (DEPTH, SSM_INNER, D_MODEL), SSM_INNER ** -0.5),
        "w_out": nrm(ks[16], (DEPTH, D_MODEL, D_MODEL), D_MODEL ** -0.5),
        "ffn_norm_w": gain(ks[17], (DEPTH, D_MODEL)),
        "ffn_w_up": nrm(ks[18], (DEPTH, D_MODEL, 2 * D_FF), D_MODEL ** -0.5),
        "ffn_conv_w": nrm(ks[19], (DEPTH, FFN_CONV, 2 * D_FF), FFN_CONV ** -0.5),
        "ffn_conv_b": nrm(ks[20], (DEPTH, 2 * D_FF), 0.01),
        "ffn_w_down": nrm(ks[21], (DEPTH, D_FF, D_MODEL), D_FF ** -0.5),
        "final_norm_w": gain(ks[22], (D_MODEL,)),
    }


def reference(x, mix_norm_w, w_in, gate_bias, gmlp_ln_w, gmlp_ln_b, gmlp_ws, gmlp_bs,
              ssm_conv_w, ssm_conv_b, ssm_dt_bias, ssm_a_log, ssm_d, ssm_norm_w,
              w_proj_a, w_proj_b, w_out, ffn_norm_w, ffn_w_up, ffn_conv_w, ffn_conv_b,
              ffn_w_down, final_norm_w):
    splits = [D_MODEL, 2 * D_MODEL, 2 * D_MODEL + 2 * GMLP_WIDTH,
              2 * D_MODEL + 2 * GMLP_WIDTH + SSM_INNER,
              2 * D_MODEL + 2 * GMLP_WIDTH + SSM_INNER + SSM_XBC]
    h = x
    for l in range(DEPTH):
        xn = rmsnorm(h, mix_norm_w[l])
        proj = xn @ w_in[l]
        g_a, g_b, za, z, xbc, dt_raw = jnp.split(proj, splits, axis=-1)
        y_a = gmlp_mixer(za, gmlp_ln_w[l], gmlp_ln_b[l], gmlp_ws[l], gmlp_bs[l]) @ w_proj_a[l]
        y_b = mamba2_mixer(z, xbc, dt_raw, ssm_conv_w[l], ssm_conv_b[l], ssm_dt_bias[l],
                           ssm_a_log[l], ssm_d[l], ssm_norm_w[l]) @ w_proj_b[l]
        merged = (jax.nn.sigmoid(g_a + gate_bias[l, 0]) * y_a
                  + jax.nn.sigmoid(g_b + gate_bias[l, 1]) * y_b)
        h = h + merged @ w_out[l]
        hn = rmsnorm(h, ffn_norm_w[l])
        up = causal_dwconv(hn @ ffn_w_up[l], ffn_conv_w[l], ffn_conv_b[l])
        gate, val = jnp.split(up, 2, axis=-1)
        h = h + (jax.nn.silu(gate) * val) @ ffn_w_down[l]
    return rmsnorm(h, final_norm_w)
```

```python
import functools

import jax
import jax.numpy as jnp
from jax import lax
from jax.experimental import pallas as pl
from jax.experimental.pallas import tpu as pltpu

F32 = jnp.float32
BF16 = jnp.bfloat16

D_MODEL = 1024
EPS = 1e-5
CHUNK = 64
GMLP_BLOCK = 128
GMLP_WIDTH = 1024
GMLP_GROUPS = 8
GMLP_GDIM = GMLP_WIDTH // GMLP_GROUPS
SSM_INNER = 2 * D_MODEL
SSM_HEAD_DIM = 64
SSM_HEADS = SSM_INNER // SSM_HEAD_DIM
SSM_GROUPS = 4
SSM_STATE = 128
SSM_CONV = 4
SSM_GROUP_WIDTH = SSM_INNER // SSM_GROUPS
SSM_XBC = SSM_INNER + 2 * SSM_GROUPS * SSM_STATE
D_FF = 2816
FFN_CONV = 3
PROJ_COLS = 2 * D_MODEL + 2 * GMLP_WIDTH + SSM_INNER + SSM_XBC

LANES = 128
SUBLANES = 8
SSD_Q = 128
VMEM_LIMIT = 56 * 1024 * 1024


def _dot(a, b):
    return jnp.dot(a, b, preferred_element_type=F32)


def _sigmoid(v):
    return 1.0 / (1.0 + jnp.exp(-v))


def _silu(v):
    return v * _sigmoid(v)


def _split3(v):
    hi = v.astype(BF16)
    r = v - hi.astype(F32)
    mid = r.astype(BF16)
    lo = (r - mid.astype(F32)).astype(BF16)
    return hi, mid, lo


def _resident(shape):
    zeros = (0,) * len(shape)
    return pl.BlockSpec(shape, lambda *_: zeros, pipeline_mode=pl.Buffered(1))


def _inproj_kernel(x_ref, nw_ref, w_ref, wdt_ref, proj_ref, dt_ref, xn_ref):
    @pl.when(pl.program_id(1) == 0)
    def _():
        x = x_ref[...]
        ms = jnp.mean(x * x, axis=-1, keepdims=True)
        xn = (x * lax.rsqrt(ms + EPS) * nw_ref[...]).astype(BF16)
        xn_ref[...] = xn
        dt_ref[...] = _dot(xn, wdt_ref[...])

    proj_ref[...] = _dot(xn_ref[...], w_ref[...]).astype(BF16)


def _in_proj(x2, norm_w, w_main, w_dt, tm, tn):
    T = x2.shape[0]
    return pl.pallas_call(
        _inproj_kernel,
        grid=(T // tm, PROJ_COLS // tn),
        in_specs=[
            pl.BlockSpec((tm, D_MODEL), lambda i, j: (i, 0)),
            pl.BlockSpec((1, D_MODEL), lambda i, j: (0, 0)),
            pl.BlockSpec((D_MODEL, tn), lambda i, j: (0, j)),
            pl.BlockSpec((D_MODEL, LANES), lambda i, j: (0, 0)),
        ],
        out_specs=[
            pl.BlockSpec((tm, tn), lambda i, j: (i, j)),
            pl.BlockSpec((tm, LANES), lambda i, j: (i, 0)),
        ],
        out_shape=[
            jax.ShapeDtypeStruct((T, PROJ_COLS), BF16),
            jax.ShapeDtypeStruct((T, LANES), F32),
        ],
        scratch_shapes=[pltpu.VMEM((tm, D_MODEL), BF16)],
        compiler_params=pltpu.CompilerParams(
            dimension_semantics=("arbitrary", "arbitrary"), vmem_limit_bytes=VMEM_LIMIT),
        name="in_proj",
    )(x2, norm_w, w_main, w_dt)


def _gmlp_kernel(za_ref, lnw_ref, lnb_ref, ws_ref, bs_ref, out_ref, wsm_ref, *, nblk):
    @pl.when(pl.program_id(0) == 0)
    def _():
        t = lax.broadcasted_iota(jnp.int32, (GMLP_BLOCK, GMLP_BLOCK), 0) // CHUNK
        s = lax.broadcasted_iota(jnp.int32, (GMLP_BLOCK, GMLP_BLOCK), 1) // CHUNK
        for g in range(GMLP_GROUPS):
            wsm_ref[g] = jnp.where(s <= t, ws_ref[g], 0.0).astype(BF16)

    for n in range(nblk):
        rows = slice(n * GMLP_BLOCK, (n + 1) * GMLP_BLOCK)
        for g in range(GMLP_GROUPS):
            cu = slice(g * GMLP_GDIM, (g + 1) * GMLP_GDIM)
            cv = slice(GMLP_WIDTH + g * GMLP_GDIM, GMLP_WIDTH + (g + 1) * GMLP_GDIM)
            v = jax.nn.gelu(za_ref[rows, cv].astype(F32))
            mu = jnp.mean(v, axis=-1, keepdims=True)
            d = v - mu
            var = jnp.mean(d * d, axis=-1, keepdims=True)
            vn = (d * lax.rsqrt(var + EPS) * lnw_ref[:, cu] + lnb_ref[:, cu]).astype(BF16)
            sv = _dot(wsm_ref[g], vn) + bs_ref[:, cu]
            u = jax.nn.gelu(za_ref[rows, cu].astype(F32))
            out_ref[rows, cu] = (u * sv).astype(BF16)


def _gmlp(proj, ln_w, ln_b, w_s, bs_full, tb):
    T = proj.shape[0]
    return pl.pallas_call(
        functools.partial(_gmlp_kernel, nblk=tb // GMLP_BLOCK),
        grid=(T // tb,),
        in_specs=[
            pl.BlockSpec((tb, 2 * GMLP_WIDTH), lambda i: (i, 1)),
            pl.BlockSpec((1, GMLP_WIDTH), lambda i: (0, 0)),
            pl.BlockSpec((1, GMLP_WIDTH), lambda i: (0, 0)),
            pl.BlockSpec((GMLP_GROUPS, GMLP_BLOCK, GMLP_BLOCK), lambda i: (0, 0, 0)),
            pl.BlockSpec((GMLP_BLOCK, GMLP_WIDTH), lambda i: (0, 0)),
        ],
        out_specs=pl.BlockSpec((tb, GMLP_WIDTH), lambda i: (i, 0)),
        out_shape=jax.ShapeDtypeStruct((T, GMLP_WIDTH), BF16),
        scratch_shapes=[pltpu.VMEM((GMLP_GROUPS, GMLP_BLOCK, GMLP_BLOCK), BF16)],
        compiler_params=pltpu.CompilerParams(
            dimension_semantics=("arbitrary",), vmem_limit_bytes=VMEM_LIMIT),
        name="gmlp",
    )(proj, ln_w, ln_b, w_s, bs_full)


def _ssd_kernel(xbc_ref, z_ref, dt_ref, cw_ref, cb_ref, dtb_ref, alog_ref, dexp_ref, nw_ref,
                e_ref, out_ref, xpad_ref, state_ref, y_ref):
    Q = SSD_Q
    HALO = SUBLANES

    @pl.when(pl.program_id(1) == 0)
    def _():
        xpad_ref[0:HALO, :] = jnp.zeros((HALO, SSM_XBC), F32)
        state_ref[...] = jnp.zeros_like(state_ref)

    xpad_ref[HALO:HALO + Q, :] = xbc_ref[...].astype(F32)
    acc = cb_ref[...] + cw_ref[0:1, :] * xpad_ref[pl.ds(HALO - 3, Q), :]
    for k in range(1, SSM_CONV):
        acc = acc + cw_ref[k:k + 1, :] * xpad_ref[pl.ds(HALO - 3 + k, Q), :]
    xpad_ref[0:HALO, :] = xpad_ref[Q:Q + HALO, :]
    xbc = _silu(acc)
    xs = xbc[:, :SSM_INNER]
    xs_b = xs.astype(BF16)
    bm = xbc[:, SSM_INNER:SSM_INNER + SSM_GROUPS * SSM_STATE].astype(BF16)
    cm = xbc[:, SSM_INNER + SSM_GROUPS * SSM_STATE:]

    dt_in = dt_ref[...] + dtb_ref[...]
    dt = jnp.maximum(dt_in, 0.0) + jnp.log(1.0 + jnp.exp(-jnp.abs(dt_in)))
    d_a = dt * (-jnp.exp(alog_ref[...]))
    row = lax.broadcasted_iota(jnp.int32, (Q, Q), 0)
    col = lax.broadcasted_iota(jnp.int32, (Q, Q), 1)
    causal = col <= row
    tri = causal.astype(BF16)
    hi, mid, lo = _split3(d_a)
    a_cum = _dot(tri, hi) + _dot(tri, mid) + _dot(tri, lo)
    a_last = a_cum[Q - 1:Q, :]
    w_end = dt * jnp.exp(a_last - a_cum)
    a_cum_t = a_cum.T
    dt_t = dt.T

    e = e_ref[...]
    w_hi = w_end.astype(BF16)
    w_lo = (w_end - w_hi.astype(F32)).astype(BF16)
    w_exp = _dot(w_hi, e) + _dot(w_lo, e)
    dl = jnp.broadcast_to(jnp.exp(a_last), (SUBLANES, LANES))
    dl_hi = dl.astype(BF16)
    dl_lo = (dl - dl_hi.astype(F32)).astype(BF16)
    dl_exp = (_dot(dl_hi, e) + _dot(dl_lo, e))[0:1, :]
    xw = (xs * w_exp).astype(BF16)

    lane = lax.broadcasted_iota(jnp.int32, (Q, LANES), 1)
    for g in range(SSM_GROUPS):
        gs = slice(g * SSM_STATE, (g + 1) * SSM_STATE)
        b_g = bm[:, gs]
        c_g = cm[:, gs]
        cb = lax.dot_general(c_g.astype(BF16), b_g, (((1,), (1,)), ((), ())),
                             preferred_element_type=F32)
        s_g = state_ref[g]
        s_gb = s_g.astype(BF16)
        for hp in range(SSM_GROUP_WIDTH // LANES):
            lanes = slice(g * SSM_GROUP_WIDTH + hp * LANES, g * SSM_GROUP_WIDTH + (hp + 1) * LANES)
            f = []
            for h in (2 * (4 * g + hp), 2 * (4 * g + hp) + 1):
                a_col = a_cum[:, h:h + 1]
                seg = a_col - a_cum_t[h:h + 1, :]
                decay = jnp.exp(jnp.where(causal, seg, -jnp.inf))
                g_in = cb * decay * dt_t[h:h + 1, :]
                c_off = c_g * jnp.exp(a_col)
                f.append(jnp.concatenate([g_in, c_off], axis=1).astype(BF16))
            lhs = jnp.concatenate(f, axis=0)
            rhs = jnp.concatenate([xs_b[:, lanes], s_gb[:, hp * LANES:(hp + 1) * LANES]], axis=0)
            o = _dot(lhs, rhs)
            y_ref[:, lanes] = jnp.where(lane < SSM_HEAD_DIM, o[:Q], o[Q:])
        gw = slice(g * SSM_GROUP_WIDTH, (g + 1) * SSM_GROUP_WIDTH)
        d_s = lax.dot_general(b_g, xw[:, gw], (((0,), (0,)), ((), ())),
                              preferred_element_type=F32)
        state_ref[g] = s_g * dl_exp[:, gw] + d_s

    y = y_ref[...] + dexp_ref[...] * xs
    y = y * _silu(z_ref[...].astype(F32))
    for g in range(SSM_GROUPS):
        gw = slice(g * SSM_GROUP_WIDTH, (g + 1) * SSM_GROUP_WIDTH)
        yg = y[:, gw]
        ms = jnp.mean(yg * yg, axis=-1, keepdims=True)
        out_ref[:, gw] = (yg * lax.rsqrt(ms + EPS) * nw_ref[:, gw]).astype(BF16)


def _ssd(proj, dt_raw, conv_w, conv_b, dt_bias, a_log, d_exp, norm_w, expand, B, S):
    Q = SSD_Q
    nc = S // Q
    return pl.pallas_call(
        _ssd_kernel,
        grid=(B, nc),
        in_specs=[
            pl.BlockSpec((Q, SSM_XBC), lambda b, c: (b * nc + c, 2)),
            pl.BlockSpec((Q, SSM_INNER), lambda b, c: (b * nc + c, 2)),
            pl.BlockSpec((Q, LANES), lambda b, c: (b * nc + c, 0)),
            pl.BlockSpec((SSM_CONV, SSM_XBC), lambda b, c: (0, 0)),
            pl.BlockSpec((1, SSM_XBC), lambda b, c: (0, 0)),
            pl.BlockSpec((1, LANES), lambda b, c: (0, 0)),
            pl.BlockSpec((1, LANES), lambda b, c: (0, 0)),
            pl.BlockSpec((1, SSM_INNER), lambda b, c: (0, 0)),
            pl.BlockSpec((1, SSM_INNER), lambda b, c: (0, 0)),
            pl.BlockSpec((LANES, SSM_INNER), lambda b, c: (0, 0)),
        ],
        out_specs=pl.BlockSpec((Q, SSM_INNER), lambda b, c: (b * nc + c, 0)),
        out_shape=jax.ShapeDtypeStruct((B * S, SSM_INNER), BF16),
        scratch_shapes=[
            pltpu.VMEM((Q + 2 * SUBLANES, SSM_XBC), F32),
            pltpu.VMEM((SSM_GROUPS, SSM_STATE, SSM_GROUP_WIDTH), F32),
            pltpu.VMEM((Q, SSM_INNER), F32),
        ],
        compiler_params=pltpu.CompilerParams(
            dimension_semantics=("arbitrary", "arbitrary"), vmem_limit_bytes=VMEM_LIMIT),
        name="ssd",
    )(proj, proj, dt_raw, conv_w, conv_b, dt_bias, a_log, d_exp, norm_w, expand)


def _merge_kernel(x_ref, g_ref, ya_ref, yb_ref, gb_ref, pa_ref, pb_ref, wo_ref, h_ref):
    g = g_ref[...].astype(F32) + gb_ref[...]
    ya = _dot(ya_ref[...], pa_ref[...])
    yb = _dot(yb_ref[...], pb_ref[...])
    m = _sigmoid(g[:, :D_MODEL]) * ya + _sigmoid(g[:, D_MODEL:]) * yb
    h_ref[...] = x_ref[...] + _dot(m.astype(BF16), wo_ref[...])


def _merge(x2, proj, ya, yb, gate_bias, w_pa, w_pb, w_o, tm):
    T = x2.shape[0]
    return pl.pallas_call(
        _merge_kernel,
        grid=(T // tm,),
        in_specs=[
            pl.BlockSpec((tm, D_MODEL), lambda i: (i, 0)),
            pl.BlockSpec((tm, 2 * D_MODEL), lambda i: (i, 0)),
            pl.BlockSpec((tm, GMLP_WIDTH), lambda i: (i, 0)),
            pl.BlockSpec((tm, SSM_INNER), lambda i: (i, 0)),
            pl.BlockSpec((1, 2 * D_MODEL), lambda i: (0, 0)),
            _resident((GMLP_WIDTH, D_MODEL)),
            _resident((SSM_INNER, D_MODEL)),
            _resident((D_MODEL, D_MODEL)),
        ],
        out_specs=pl.BlockSpec((tm, D_MODEL), lambda i: (i, 0)),
        out_shape=jax.ShapeDtypeStruct((T, D_MODEL), F32),
        compiler_params=pltpu.CompilerParams(
            dimension_semantics=("arbitrary",), vmem_limit_bytes=VMEM_LIMIT),
        name="merge",
    )(x2, proj, ya, yb, gate_bias, w_pa, w_pb, w_o)


def _ffn_kernel(h_ref, nw_ref, wup_ref, cw_ref, cb_ref, wdn_ref, fw_ref, out_ref,
                up_ref, act_ref, *, tm):
    HALO = SUBLANES

    @pl.when(pl.program_id(1) == 0)
    def _():
        up_ref[0:HALO, :] = jnp.zeros((HALO, 2 * D_FF), F32)

    h = h_ref[...]
    ms = jnp.mean(h * h, axis=-1, keepdims=True)
    hn = (h * lax.rsqrt(ms + EPS) * nw_ref[...]).astype(BF16)
    up_ref[HALO:HALO + tm, :] = _dot(hn, wup_ref[...])

    def conv(cols):
        acc = cb_ref[:, cols] + cw_ref[0:1, cols] * up_ref[pl.ds(HALO - 2, tm), cols]
        for k in range(1, FFN_CONV):
            acc = acc + cw_ref[k:k + 1, cols] * up_ref[pl.ds(HALO - 2 + k, tm), cols]
        return acc

    for j in range(D_FF // LANES):
        gate = conv(slice(j * LANES, (j + 1) * LANES))
        val = conv(slice(D_FF + j * LANES, D_FF + (j + 1) * LANES))
        act_ref[:, j * LANES:(j + 1) * LANES] = (_silu(gate) * val).astype(BF16)
    up_ref[0:HALO, :] = up_ref[tm:tm + HALO, :]

    h2 = h + _dot(act_ref[...], wdn_ref[...])
    ms2 = jnp.mean(h2 * h2, axis=-1, keepdims=True)
    out_ref[...] = h2 * lax.rsqrt(ms2 + EPS) * fw_ref[...]


def _ffn(h1, norm_w, w_up, conv_w, conv_b, w_dn, final_w, B, S, tm):
    nt = S // tm
    return pl.pallas_call(
        functools.partial(_ffn_kernel, tm=tm),
        grid=(B, nt),
        in_specs=[
            pl.BlockSpec((tm, D_MODEL), lambda b, t: (b * nt + t, 0)),
            pl.BlockSpec((1, D_MODEL), lambda b, t: (0, 0)),
            _resident((D_MODEL, 2 * D_FF)),
            pl.BlockSpec((FFN_CONV, 2 * D_FF), lambda b, t: (0, 0)),
            pl.BlockSpec((1, 2 * D_FF), lambda b, t: (0, 0)),
            _resident((D_FF, D_MODEL)),
            pl.BlockSpec((1, D_MODEL), lambda b, t: (0, 0)),
        ],
        out_specs=pl.BlockSpec((tm, D_MODEL), lambda b, t: (b * nt + t, 0)),
        out_shape=jax.ShapeDtypeStruct((B * S, D_MODEL), F32),
        scratch_shapes=[
            pltpu.VMEM((tm + 2 * SUBLANES, 2 * D_FF), F32),
            pltpu.VMEM((tm, D_FF), BF16),
        ],
        compiler_params=pltpu.CompilerParams(
            dimension_semantics=("arbitrary", "arbitrary"), vmem_limit_bytes=VMEM_LIMIT),
        name="ffn",
    )(h1, norm_w, w_up, conv_w, conv_b, w_dn, final_w)


def _pad_lanes(v):
    return jnp.pad(v.astype(F32), (0, LANES - v.shape[0])).reshape(1, LANES)


def _tile(n, want):
    t = min(n, want)
    assert n % t == 0, (n, want)
    return t


def kernel(x, mix_norm_w, w_in, gate_bias, gmlp_ln_w, gmlp_ln_b, gmlp_ws, gmlp_bs,
           ssm_conv_w, ssm_conv_b, ssm_dt_bias, ssm_a_log, ssm_d, ssm_norm_w,
           w_proj_a, w_proj_b, w_out, ffn_norm_w, ffn_w_up, ffn_conv_w, ffn_conv_b,
           ffn_w_down, final_norm_w):
    B, S, _ = x.shape
    T = B * S
    assert S % SSD_Q == 0 and S % GMLP_BLOCK == 0
    assert w_in.shape[0] == 1, "single-layer trunk: the final norm is fused into the FFN kernel"
    l = 0
    expand = (jnp.arange(LANES)[:, None] == jnp.arange(SSM_INNER)[None, :] // SSM_HEAD_DIM).astype(BF16)

    x2 = x.reshape(T, D_MODEL)
    w_in_b = w_in[l].astype(BF16)
    w_dt = jnp.pad(w_in_b[:, PROJ_COLS:], ((0, 0), (0, LANES - SSM_HEADS)))
    proj, dt_raw = _in_proj(x2, mix_norm_w[l].reshape(1, D_MODEL), w_in_b[:, :PROJ_COLS], w_dt,
                            _tile(T, 1024), 1024)
    bs_full = jnp.repeat(gmlp_bs[l].T, GMLP_GDIM, axis=1)
    ya = _gmlp(proj, gmlp_ln_w[l].reshape(1, GMLP_WIDTH), gmlp_ln_b[l].reshape(1, GMLP_WIDTH),
               gmlp_ws[l], bs_full, _tile(T, 512))
    yb = _ssd(proj, dt_raw, ssm_conv_w[l], ssm_conv_b[l].reshape(1, SSM_XBC),
              _pad_lanes(ssm_dt_bias[l]), _pad_lanes(ssm_a_log[l]),
              jnp.repeat(ssm_d[l].astype(F32), SSM_HEAD_DIM).reshape(1, SSM_INNER),
              ssm_norm_w[l].reshape(1, SSM_INNER), expand, B, S)
    h1 = _merge(x2, proj, ya, yb, gate_bias[l].reshape(1, 2 * D_MODEL),
                w_proj_a[l].astype(BF16), w_proj_b[l].astype(BF16), w_out[l].astype(BF16),
                _tile(T, 512))
    out = _ffn(h1, ffn_norm_w[l].reshape(1, D_MODEL), ffn_w_up[l].astype(BF16), ffn_conv_w[l],
               ffn_conv_b[l].reshape(1, 2 * D_FF), ffn_w_down[l].astype(BF16),
               final_norm_w.reshape(1, D_MODEL), B, S, _tile(S, 256))
    return out.reshape(B, S, D_MODEL)
```

```python
import functools
import math

import jax
import jax.numpy as jnp
from jax import lax
from jax.experimental import pallas as pl
from jax.experimental.pallas import tpu as pltpu

F32 = jnp.float32
BF16 = jnp.bfloat16

D_MODEL = 1024
EPS = 1e-5
CHUNK = 64
GMLP_BLOCK = 128
GMLP_WIDTH = 1024
GMLP_GROUPS = 8
GMLP_GDIM = GMLP_WIDTH // GMLP_GROUPS
SSM_INNER = 2 * D_MODEL
SSM_HEAD_DIM = 64
SSM_HEADS = SSM_INNER // SSM_HEAD_DIM
SSM_GROUPS = 4
SSM_STATE = 128
SSM_CONV = 4
SSM_GROUP_WIDTH = SSM_INNER // SSM_GROUPS
SSM_BC = SSM_GROUPS * SSM_STATE
SSM_XBC = SSM_INNER + 2 * SSM_BC
D_FF = 2816
FFN_CONV = 3
COL_GATES = 0
COL_ZA = 2 * D_MODEL
COL_Z = COL_ZA + 2 * GMLP_WIDTH
COL_XBC = COL_Z + SSM_INNER
COL_DT = COL_XBC + SSM_XBC

LANES = 128
SUBLANES = 8
SSD_Q = 128
MXU_ROWS = 256
FFN_COLS = 256
FFN_LOOKAHEAD = 3
VMEM_LIMIT = 56 * 1024 * 1024
LOG2E = math.log2(math.e)


def _dot(a, b):
    return jnp.dot(a, b, preferred_element_type=F32)


def _sigmoid(v):
    return 1.0 / (1.0 + jnp.exp2(v * (-LOG2E)))


def _silu(v):
    return v * _sigmoid(v)


def _causal_conv(prev_tail, cur, w, b):
    n = cur.shape[0]
    taps = w.shape[0]
    padded = jnp.concatenate([prev_tail, cur], axis=0)
    body = slice(SUBLANES, SUBLANES + n)
    if taps == 4:
        back1 = pltpu.roll(padded, 1, axis=0)
        near = w[3:4] * cur + w[2:3] * back1[body]
        far = w[1:2] * padded + w[0:1] * back1
        return b + near + pltpu.roll(far, 2, axis=0)[body]
    acc = b + w[taps - 1:taps] * cur
    for k in range(1, taps):
        shifted = pltpu.roll(padded, k, axis=0)[body]
        acc = acc + w[taps - 1 - k:taps - k] * shifted
    return acc


def _rmsnorm(v, w):
    ms = jnp.mean(v * v, axis=-1, keepdims=True)
    return v * lax.rsqrt(ms + EPS) * w


def _split3(v):
    hi = v.astype(BF16)
    r = v - hi.astype(F32)
    mid = r.astype(BF16)
    lo = (r - mid.astype(F32)).astype(BF16)
    return hi, mid, lo


def _hi_lo(v):
    hi = v.astype(BF16)
    lo = (v - hi.astype(F32)).astype(BF16)
    return jnp.concatenate([hi, lo], axis=1)


def _resident(shape):
    zeros = (0,) * len(shape)
    return pl.BlockSpec(shape, lambda *_: zeros, pipeline_mode=pl.Buffered(1))


def _params(n_axes):
    return pltpu.CompilerParams(dimension_semantics=("arbitrary",) * n_axes,
                                vmem_limit_bytes=VMEM_LIMIT)


def _gmlp_kernel(x_ref, nw_ref, wza_ref, lnw_ref, lnb_ref, ws_ref, bs_ref, out_ref, wsm_ref, *, tm):
    @pl.when(pl.program_id(0) == 0)
    def _():
        t = lax.broadcasted_iota(jnp.int32, (GMLP_BLOCK, GMLP_BLOCK), 0) // CHUNK
        s = lax.broadcasted_iota(jnp.int32, (GMLP_BLOCK, GMLP_BLOCK), 1) // CHUNK
        for g in range(GMLP_GROUPS):
            wsm_ref[g] = jnp.where(s <= t, ws_ref[g], 0.0).astype(BF16)

    xn = _rmsnorm(x_ref[...], nw_ref[...]).astype(BF16)
    for r in range(tm // MXU_ROWS):
        za = _dot(xn[r * MXU_ROWS:(r + 1) * MXU_ROWS], wza_ref[...])
        for n in range(MXU_ROWS // GMLP_BLOCK):
            rows = slice(n * GMLP_BLOCK, (n + 1) * GMLP_BLOCK)
            orow = slice(r * MXU_ROWS + n * GMLP_BLOCK, r * MXU_ROWS + (n + 1) * GMLP_BLOCK)
            for g in range(GMLP_GROUPS):
                cu = slice(g * GMLP_GDIM, (g + 1) * GMLP_GDIM)
                cv = slice(GMLP_WIDTH + g * GMLP_GDIM, GMLP_WIDTH + (g + 1) * GMLP_GDIM)
                v = jax.nn.gelu(za[rows, cv])
                mu = jnp.mean(v, axis=-1, keepdims=True)
                d = v - mu
                var = jnp.mean(d * d, axis=-1, keepdims=True)
                vn = (d * lax.rsqrt(var + EPS) * lnw_ref[:, cu] + lnb_ref[:, cu]).astype(BF16)
                sv = _dot(wsm_ref[g], vn) + bs_ref[:, cu]
                out_ref[orow, cu] = (jax.nn.gelu(za[rows, cu]) * sv).astype(BF16)


def _gmlp_branch(x2, norm_w, w_za, ln_w, ln_b, w_s, bs_full, tm):
    T = x2.shape[0]
    return pl.pallas_call(
        functools.partial(_gmlp_kernel, tm=tm),
        grid=(T // tm,),
        in_specs=[
            pl.BlockSpec((tm, D_MODEL), lambda i: (i, 0)),
            _resident((1, D_MODEL)),
            _resident((D_MODEL, 2 * GMLP_WIDTH)),
            _resident((1, GMLP_WIDTH)),
            _resident((1, GMLP_WIDTH)),
            _resident((GMLP_GROUPS, GMLP_BLOCK, GMLP_BLOCK)),
            _resident((GMLP_BLOCK, GMLP_WIDTH)),
        ],
        out_specs=pl.BlockSpec((tm, GMLP_WIDTH), lambda i: (i, 0)),
        out_shape=jax.ShapeDtypeStruct((T, GMLP_WIDTH), BF16),
        scratch_shapes=[pltpu.VMEM((GMLP_GROUPS, GMLP_BLOCK, GMLP_BLOCK), BF16)],
        compiler_params=_params(1),
        name="gmlp_branch",
    )(x2, norm_w, w_za, ln_w, ln_b, w_s, bs_full)


def _ssd_chunk(xbc, z, dt_in, alog, dexp, gnw, e2, state_ref):
    Q = SSD_Q
    xs = xbc[:, :SSM_INNER]
    xs_b = xs.astype(BF16)
    bm = xbc[:, SSM_INNER:SSM_INNER + SSM_BC].astype(BF16)
    cm = xbc[:, SSM_INNER + SSM_BC:]

    dt = jnp.maximum(dt_in, 0.0) + jnp.log(1.0 + jnp.exp(-jnp.abs(dt_in)))
    d_a = dt * (-jnp.exp(alog))
    row = lax.broadcasted_iota(jnp.int32, (Q, Q), 0)
    col = lax.broadcasted_iota(jnp.int32, (Q, Q), 1)
    causal = col <= row
    tri = causal.astype(BF16)
    hi, mid, lo = _split3(d_a)
    a2 = (_dot(tri, hi) + _dot(tri, mid) + _dot(tri, lo)) * LOG2E
    a2_last = a2[Q - 1:Q, :]
    ea = jnp.exp2(a2)
    w_end = dt * jnp.exp2(a2_last - a2)
    r2_t = a2.T - jnp.log(dt.T) * LOG2E

    wide = _dot(jnp.concatenate([_hi_lo(w_end), _hi_lo(jnp.broadcast_to(jnp.exp2(a2_last), (2 * SUBLANES, LANES)))],
                                axis=0), e2)
    xw = (xs * wide[:Q]).astype(BF16)
    chunk_decay = wide[Q:Q + 1]

    lane = lax.broadcasted_iota(jnp.int32, (Q, LANES), 1)
    out = []
    for g in range(SSM_GROUPS):
        gs = slice(g * SSM_STATE, (g + 1) * SSM_STATE)
        b_g = bm[:, gs]
        c_g = cm[:, gs]
        cb = lax.dot_general(c_g.astype(BF16), b_g, (((1,), (1,)), ((), ())),
                             preferred_element_type=F32)
        s_g = state_ref[g]
        s_gb = s_g.astype(BF16)
        y_pairs = []
        for hp in range(SSM_GROUP_WIDTH // LANES):
            lanes = slice(g * SSM_GROUP_WIDTH + hp * LANES, g * SSM_GROUP_WIDTH + (hp + 1) * LANES)
            f = []
            for h in (2 * (4 * g + hp), 2 * (4 * g + hp) + 1):
                seg = a2[:, h:h + 1] - r2_t[h:h + 1, :]
                g_in = cb * jnp.exp2(jnp.where(causal, seg, -jnp.inf))
                c_off = c_g * ea[:, h:h + 1]
                f.append(jnp.concatenate([g_in, c_off], axis=1).astype(BF16))
            lhs = jnp.concatenate(f, axis=0)
            rhs = jnp.concatenate([xs_b[:, lanes], s_gb[:, hp * LANES:(hp + 1) * LANES]], axis=0)
            o = _dot(lhs, rhs)
            y_pairs.append(jnp.where(lane < SSM_HEAD_DIM, o[:Q], o[Q:]))
        gw = slice(g * SSM_GROUP_WIDTH, (g + 1) * SSM_GROUP_WIDTH)
        d_s = lax.dot_general(b_g, xw[:, gw], (((0,), (0,)), ((), ())),
                              preferred_element_type=F32)
        state_ref[g] = s_g * chunk_decay[:, gw] + d_s
        y_g = (jnp.concatenate(y_pairs, axis=1) + dexp[:, gw] * xs[:, gw]) * _silu(z[:, gw])
        out.append(_rmsnorm(y_g, gnw[:, gw]).astype(BF16))
    return jnp.concatenate(out, axis=1)


def _ssd_kernel(x_ref, nw_ref, wz_ref, wx_ref, wdt_ref, cw_ref, cb_ref, dtb_ref, alog_ref, dexp_ref,
                gnw_ref, e2_ref, pb_ref, out_ref, tail_ref, state_ref, *, tl):
    Q = SSD_Q

    @pl.when(pl.program_id(1) == 0)
    def _():
        tail_ref[...] = jnp.zeros_like(tail_ref)
        state_ref[...] = jnp.zeros_like(state_ref)

    xn = _rmsnorm(x_ref[...], nw_ref[...]).astype(BF16)
    z = _dot(xn, wz_ref[...])
    dt_in = _dot(xn, wdt_ref[...]) + dtb_ref[...]
    xraw = _dot(xn, wx_ref[...])
    xbc = _silu(_causal_conv(tail_ref[...], xraw, cw_ref[...], cb_ref[...]))
    tail_ref[...] = xraw[tl - SUBLANES:tl]
    yn = []
    for c in range(tl // Q):
        rows = slice(c * Q, (c + 1) * Q)
        yn.append(_ssd_chunk(xbc[rows], z[rows], dt_in[rows], alog_ref[...], dexp_ref[...], gnw_ref[...],
                             e2_ref[...], state_ref))
    out_ref[...] = _dot(jnp.concatenate(yn, axis=0), pb_ref[...]).astype(BF16)


def _ssd_branch(x2, norm_w, w_z, w_xbc, w_dt, conv_w, conv_b, dt_bias, a_log, d_exp, gnorm_w, e2, w_pb,
                B, S, tl):
    nt = S // tl
    return pl.pallas_call(
        functools.partial(_ssd_kernel, tl=tl),
        grid=(B, nt),
        in_specs=[
            pl.BlockSpec((tl, D_MODEL), lambda b, t: (b * nt + t, 0)),
            _resident((1, D_MODEL)),
            _resident((D_MODEL, SSM_INNER)),
            _resident((D_MODEL, SSM_XBC)),
            _resident((D_MODEL, LANES)),
            _resident((SSM_CONV, SSM_XBC)),
            _resident((1, SSM_XBC)),
            _resident((1, LANES)),
            _resident((1, LANES)),
            _resident((1, SSM_INNER)),
            _resident((1, SSM_INNER)),
            _resident((2 * LANES, SSM_INNER)),
            _resident((SSM_INNER, D_MODEL)),
        ],
        out_specs=pl.BlockSpec((tl, D_MODEL), lambda b, t: (b * nt + t, 0)),
        out_shape=jax.ShapeDtypeStruct((B * S, D_MODEL), BF16),
        scratch_shapes=[
            pltpu.VMEM((SUBLANES, SSM_XBC), F32),
            pltpu.VMEM((SSM_GROUPS, SSM_STATE, SSM_GROUP_WIDTH), F32),
        ],
        compiler_params=_params(2),
        name="ssd_branch",
    )(x2, norm_w, w_z, w_xbc, w_dt, conv_w, conv_b, dt_bias, a_log, d_exp, gnorm_w, e2, w_pb)


def _merge_kernel(x_ref, nw_ref, wg_ref, gb_ref, ya_ref, ybp_ref, pa_ref, wo_ref, h_ref, *, tm):
    for r in range(tm // MXU_ROWS):
        rows = slice(r * MXU_ROWS, (r + 1) * MXU_ROWS)
        x = x_ref[rows, :]
        xn = _rmsnorm(x, nw_ref[...]).astype(BF16)
        g = _dot(xn, wg_ref[...]) + gb_ref[...]
        m = (_sigmoid(g[:, :D_MODEL]) * _dot(ya_ref[rows, :], pa_ref[...])
             + _sigmoid(g[:, D_MODEL:]) * ybp_ref[rows, :].astype(F32))
        h_ref[rows, :] = x + _dot(m.astype(BF16), wo_ref[...])


def _merge(x2, norm_w, w_g, gate_bias, ya, ybp, w_pa, w_o, tm):
    T = x2.shape[0]
    return pl.pallas_call(
        functools.partial(_merge_kernel, tm=tm),
        grid=(T // tm,),
        in_specs=[
            pl.BlockSpec((tm, D_MODEL), lambda i: (i, 0)),
            _resident((1, D_MODEL)),
            _resident((D_MODEL, 2 * D_MODEL)),
            _resident((1, 2 * D_MODEL)),
            pl.BlockSpec((tm, GMLP_WIDTH), lambda i: (i, 0)),
            pl.BlockSpec((tm, D_MODEL), lambda i: (i, 0)),
            _resident((GMLP_WIDTH, D_MODEL)),
            _resident((D_MODEL, D_MODEL)),
        ],
        out_specs=pl.BlockSpec((tm, D_MODEL), lambda i: (i, 0)),
        out_shape=jax.ShapeDtypeStruct((T, D_MODEL), F32),
        compiler_params=_params(1),
        name="merge",
    )(x2, norm_w, w_g, gate_bias, ya, ybp, w_pa, w_o)


def _ffn_kernel(h_ref, nw_ref, wup_ref, cw_ref, cb_ref, wdn_ref, fw_ref, out_ref, tail_ref, *, tm):
    W = FFN_COLS

    @pl.when(pl.program_id(1) == 0)
    def _():
        tail_ref[...] = jnp.zeros_like(tail_ref)

    h = h_ref[...]
    hn = _rmsnorm(h, nw_ref[...]).astype(BF16)
    tail = tail_ref[...]

    def up_proj(j):
        ups = []
        for base in (0, D_FF):
            cols = slice(base + j * W, base + (j + 1) * W)
            ups.append((cols, _dot(hn, wup_ref[:, cols])))
        return ups

    n_stage = D_FF // W
    acc_out = h
    ups = [up_proj(j) for j in range(FFN_LOOKAHEAD)]
    for j in range(n_stage):
        if j + FFN_LOOKAHEAD < n_stage:
            ups.append(up_proj(j + FFN_LOOKAHEAD))
        conv = []
        for cols, up in ups.pop(0):
            conv.append(_causal_conv(tail[:, cols], up, cw_ref[:, cols], cb_ref[:, cols]))
            tail_ref[:, cols] = up[tm - SUBLANES:tm]
        act = (_silu(conv[0]) * conv[1]).astype(BF16)
        acc_out = acc_out + _dot(act, wdn_ref[j * W:(j + 1) * W, :])
    out_ref[...] = _rmsnorm(acc_out, fw_ref[...])


def _ffn(h1, norm_w, w_up, conv_w, conv_b, w_dn, final_w, B, S, tm):
    nt = S // tm
    return pl.pallas_call(
        functools.partial(_ffn_kernel, tm=tm),
        grid=(B, nt),
        in_specs=[
            pl.BlockSpec((tm, D_MODEL), lambda b, t: (b * nt + t, 0)),
            _resident((1, D_MODEL)),
            _resident((D_MODEL, 2 * D_FF)),
            _resident((FFN_CONV, 2 * D_FF)),
            _resident((1, 2 * D_FF)),
            _resident((D_FF, D_MODEL)),
            _resident((1, D_MODEL)),
        ],
        out_specs=pl.BlockSpec((tm, D_MODEL), lambda b, t: (b * nt + t, 0)),
        out_shape=jax.ShapeDtypeStruct((B * S, D_MODEL), F32),
        scratch_shapes=[pltpu.VMEM((SUBLANES, 2 * D_FF), F32)],
        compiler_params=_params(2),
        name="ffn",
    )(h1, norm_w, w_up, conv_w, conv_b, w_dn, final_w)


def _pad_lanes(v):
    return jnp.pad(v.astype(F32), (0, LANES - v.shape[0])).reshape(1, LANES)


def _tile(n, want):
    t = min(n, want)
    assert n % t == 0, (n, want)
    return t


def kernel(x, mix_norm_w, w_in, gate_bias, gmlp_ln_w, gmlp_ln_b, gmlp_ws, gmlp_bs,
           ssm_conv_w, ssm_conv_b, ssm_dt_bias, ssm_a_log, ssm_d, ssm_norm_w,
           w_proj_a, w_proj_b, w_out, ffn_norm_w, ffn_w_up, ffn_conv_w, ffn_conv_b,
           ffn_w_down, final_norm_w):
    B, S, _ = x.shape
    T = B * S
    assert S % MXU_ROWS == 0
    assert w_in.shape[0] == 1, "single-layer trunk: the final norm is fused into the FFN kernel"
    l = 0
    head_of_col = jnp.arange(SSM_INNER)[None, :] // SSM_HEAD_DIM
    expand = (jnp.arange(LANES)[:, None] == head_of_col).astype(BF16)
    expand2 = jnp.concatenate([expand, expand], axis=0)

    x2 = x.reshape(T, D_MODEL)
    w_in_b = w_in[l].astype(BF16)
    mix_w = mix_norm_w[l].reshape(1, D_MODEL)
    w_dt = jnp.pad(w_in_b[:, COL_DT:], ((0, 0), (0, LANES - SSM_HEADS)))
    bs_full = jnp.repeat(gmlp_bs[l].T, GMLP_GDIM, axis=1)

    ya = _gmlp_branch(x2, mix_w, w_in_b[:, COL_ZA:COL_Z], gmlp_ln_w[l].reshape(1, GMLP_WIDTH),
                      gmlp_ln_b[l].reshape(1, GMLP_WIDTH), gmlp_ws[l], bs_full, _tile(T, 512))
    ybp = _ssd_branch(x2, mix_w, w_in_b[:, COL_Z:COL_XBC], w_in_b[:, COL_XBC:COL_DT], w_dt,
                      ssm_conv_w[l], ssm_conv_b[l].reshape(1, SSM_XBC),
                      _pad_lanes(ssm_dt_bias[l]), _pad_lanes(ssm_a_log[l]),
                      jnp.repeat(ssm_d[l].astype(F32), SSM_HEAD_DIM).reshape(1, SSM_INNER),
                      ssm_norm_w[l].reshape(1, SSM_INNER), expand2, w_proj_b[l].astype(BF16),
                      B, S, _tile(S, 256))
    h1 = _merge(x2, mix_w, w_in_b[:, COL_GATES:COL_ZA], gate_bias[l].reshape(1, 2 * D_MODEL), ya, ybp,
                w_proj_a[l].astype(BF16), w_out[l].astype(BF16), _tile(T, 512))
    out = _ffn(h1, ffn_norm_w[l].reshape(1, D_MODEL), ffn_w_up[l].astype(BF16), ffn_conv_w[l],
               ffn_conv_b[l].reshape(1, 2 * D_FF), ffn_w_down[l].astype(BF16),
               final_norm_w.reshape(1, D_MODEL), B, S, _tile(S, 256))
    return out.reshape(B, S, D_MODEL)
```

```python
import functools
import math

import jax
import jax.numpy as jnp
from jax import lax
from jax.experimental import pallas as pl
from jax.experimental.pallas import tpu as pltpu

F32 = jnp.float32
BF16 = jnp.bfloat16

D_MODEL = 1024
EPS = 1e-5
CHUNK = 64
GMLP_BLOCK = 128
GMLP_WIDTH = 1024
GMLP_GROUPS = 8
GMLP_GDIM = GMLP_WIDTH // GMLP_GROUPS
SSM_INNER = 2 * D_MODEL
SSM_HEAD_DIM = 64
SSM_HEADS = SSM_INNER // SSM_HEAD_DIM
SSM_GROUPS = 4
SSM_STATE = 128
SSM_CONV = 4
SSM_GROUP_WIDTH = SSM_INNER // SSM_GROUPS
SSM_BC = SSM_GROUPS * SSM_STATE
SSM_XBC = SSM_INNER + 2 * SSM_BC
D_FF = 2816
FFN_CONV = 3
COL_GATES = 0
COL_ZA = 2 * D_MODEL
COL_Z = COL_ZA + 2 * GMLP_WIDTH
COL_XBC = COL_Z + SSM_INNER
COL_DT = COL_XBC + SSM_XBC

LANES = 128
SUBLANES = 8
SSD_Q = 128
MXU_ROWS = 256
PROJ_PIECE = 512
SSD_FRONT_PIECES = 5
GMLP_LOOKAHEAD = 2
FFN_COLS = 256
FFN_LOOKAHEAD = 3
VMEM_LIMIT = 56 * 1024 * 1024
LOG2E = math.log2(math.e)


def _dot(a, b):
    return jnp.dot(a, b, preferred_element_type=F32)


def _sigmoid(v):
    return 1.0 / (1.0 + jnp.exp2(v * (-LOG2E)))


def _silu(v):
    return v * _sigmoid(v)


def _causal_conv(prev_tail, cur, w, b):
    n = cur.shape[0]
    taps = w.shape[0]
    padded = jnp.concatenate([prev_tail, cur], axis=0)
    body = slice(SUBLANES, SUBLANES + n)
    if taps == 4:
        back1 = pltpu.roll(padded, 1, axis=0)
        near = w[3:4] * cur + w[2:3] * back1[body]
        far = w[1:2] * padded + w[0:1] * back1
        return b + near + pltpu.roll(far, 2, axis=0)[body]
    acc = b + w[taps - 1:taps] * cur
    for k in range(1, taps):
        shifted = pltpu.roll(padded, k, axis=0)[body]
        acc = acc + w[taps - 1 - k:taps - k] * shifted
    return acc


def _rmsnorm(v, w):
    ms = jnp.mean(v * v, axis=-1, keepdims=True)
    return v * lax.rsqrt(ms + EPS) * w


def _split3(v):
    hi = v.astype(BF16)
    r = v - hi.astype(F32)
    mid = r.astype(BF16)
    lo = (r - mid.astype(F32)).astype(BF16)
    return hi, mid, lo


def _hi_lo(v):
    hi = v.astype(BF16)
    lo = (v - hi.astype(F32)).astype(BF16)
    return jnp.concatenate([hi, lo], axis=1)


def _resident(shape):
    zeros = (0,) * len(shape)
    return pl.BlockSpec(shape, lambda *_: zeros, pipeline_mode=pl.Buffered(1))


def _params(n_axes):
    return pltpu.CompilerParams(dimension_semantics=("arbitrary",) * n_axes,
                                vmem_limit_bytes=VMEM_LIMIT)


def _gmlp_kernel(x_ref, nw_ref, wza_ref, lnw_ref, lnb_ref, ws_ref, bs_ref, out_ref, wsm_ref, *, tm):
    @pl.when(pl.program_id(0) == 0)
    def _():
        t = lax.broadcasted_iota(jnp.int32, (GMLP_BLOCK, GMLP_BLOCK), 0) // CHUNK
        s = lax.broadcasted_iota(jnp.int32, (GMLP_BLOCK, GMLP_BLOCK), 1) // CHUNK
        for g in range(GMLP_GROUPS):
            wsm_ref[g] = jnp.where(s <= t, ws_ref[g], 0.0).astype(BF16)

    xn = _rmsnorm(x_ref[...], nw_ref[...]).astype(BF16)
    PAIR = 2 * GMLP_GDIM

    def proj(r, gp):
        rows = slice(r * MXU_ROWS, (r + 1) * MXU_ROWS)
        return (_dot(xn[rows], wza_ref[:, gp * PAIR:(gp + 1) * PAIR]),
                _dot(xn[rows], wza_ref[:, GMLP_WIDTH + gp * PAIR:GMLP_WIDTH + (gp + 1) * PAIR]))

    stages = [(r, gp) for r in range(tm // MXU_ROWS) for gp in range(GMLP_GROUPS // 2)]
    pending = [proj(*st) for st in stages[:GMLP_LOOKAHEAD]]
    for i, (r, gp) in enumerate(stages):
        if i + GMLP_LOOKAHEAD < len(stages):
            pending.append(proj(*stages[i + GMLP_LOOKAHEAD]))
        zu, zv = pending.pop(0)
        for n in range(MXU_ROWS // GMLP_BLOCK):
            rows = slice(n * GMLP_BLOCK, (n + 1) * GMLP_BLOCK)
            orow = slice(r * MXU_ROWS + n * GMLP_BLOCK, r * MXU_ROWS + (n + 1) * GMLP_BLOCK)
            for k in range(2):
                g = 2 * gp + k
                cg = slice(g * GMLP_GDIM, (g + 1) * GMLP_GDIM)
                ck = slice(k * GMLP_GDIM, (k + 1) * GMLP_GDIM)
                v = jax.nn.gelu(zv[rows, ck])
                mu = jnp.mean(v, axis=-1, keepdims=True)
                d = v - mu
                var = jnp.mean(d * d, axis=-1, keepdims=True)
                vn = (d * lax.rsqrt(var + EPS) * lnw_ref[:, cg] + lnb_ref[:, cg]).astype(BF16)
                sv = _dot(wsm_ref[g], vn) + bs_ref[:, cg]
                out_ref[orow, cg] = (jax.nn.gelu(zu[rows, ck]) * sv).astype(BF16)


def _gmlp_branch(x2, norm_w, w_za, ln_w, ln_b, w_s, bs_full, tm):
    T = x2.shape[0]
    return pl.pallas_call(
        functools.partial(_gmlp_kernel, tm=tm),
        grid=(T // tm,),
        in_specs=[
            pl.BlockSpec((tm, D_MODEL), lambda i: (i, 0)),
            _resident((1, D_MODEL)),
            _resident((D_MODEL, 2 * GMLP_WIDTH)),
            _resident((1, GMLP_WIDTH)),
            _resident((1, GMLP_WIDTH)),
            _resident((GMLP_GROUPS, GMLP_BLOCK, GMLP_BLOCK)),
            _resident((GMLP_BLOCK, GMLP_WIDTH)),
        ],
        out_specs=pl.BlockSpec((tm, GMLP_WIDTH), lambda i: (i, 0)),
        out_shape=jax.ShapeDtypeStruct((T, GMLP_WIDTH), BF16),
        scratch_shapes=[pltpu.VMEM((GMLP_GROUPS, GMLP_BLOCK, GMLP_BLOCK), BF16)],
        compiler_params=_params(1),
        name="gmlp_branch",
    )(x2, norm_w, w_za, ln_w, ln_b, w_s, bs_full)


def _ssd_decay(dt_in, alog, e2):
    Q = SSD_Q
    dt = jnp.maximum(dt_in, 0.0) + jnp.log(1.0 + jnp.exp(-jnp.abs(dt_in)))
    d_a = dt * (-jnp.exp(alog))
    row = lax.broadcasted_iota(jnp.int32, (Q, Q), 0)
    col = lax.broadcasted_iota(jnp.int32, (Q, Q), 1)
    tri = (col <= row).astype(BF16)
    hi, mid, lo = _split3(d_a)
    a2 = (_dot(tri, hi) + _dot(tri, mid) + _dot(tri, lo)) * LOG2E
    a2_last = a2[Q - 1:Q, :]
    ea = jnp.exp2(a2)
    w_end = dt * jnp.exp2(a2_last - a2)
    r2_t = a2.T - jnp.log(dt.T) * LOG2E
    wide = _dot(jnp.concatenate([_hi_lo(w_end), _hi_lo(jnp.broadcast_to(jnp.exp2(a2_last), (2 * SUBLANES, LANES)))],
                                axis=0), e2)
    return a2, ea, r2_t, wide[:Q], wide[Q:Q + 1]


def _ssd_scan(decay, cbs, xs_b, bm, cm, gate, dexp, gnw, state_ref, tick):
    Q = SSD_Q
    a2, ea, r2_t, w_end, chunk_decay = decay
    row = lax.broadcasted_iota(jnp.int32, (Q, Q), 0)
    col = lax.broadcasted_iota(jnp.int32, (Q, Q), 1)
    causal = col <= row
    lane = lax.broadcasted_iota(jnp.int32, (Q, LANES), 1)
    out = []
    for g in range(SSM_GROUPS):
        gs = slice(g * SSM_STATE, (g + 1) * SSM_STATE)
        c_g = cm[:, gs]
        s_g = state_ref[g]
        s_gb = s_g.astype(BF16)
        y_pairs = []
        for hp in range(SSM_GROUP_WIDTH // LANES):
            lanes = slice(g * SSM_GROUP_WIDTH + hp * LANES, g * SSM_GROUP_WIDTH + (hp + 1) * LANES)
            f = []
            for h in (2 * (4 * g + hp), 2 * (4 * g + hp) + 1):
                seg = a2[:, h:h + 1] - r2_t[h:h + 1, :]
                g_in = cbs[g] * jnp.exp2(jnp.where(causal, seg, -jnp.inf))
                c_off = c_g * ea[:, h:h + 1]
                f.append(jnp.concatenate([g_in, c_off], axis=1).astype(BF16))
            lhs = jnp.concatenate(f, axis=0)
            rhs = jnp.concatenate([xs_b[:, lanes], s_gb[:, hp * LANES:(hp + 1) * LANES]], axis=0)
            o = _dot(lhs, rhs)
            y_pairs.append(jnp.where(lane < SSM_HEAD_DIM, o[:Q], o[Q:]))
        gw = slice(g * SSM_GROUP_WIDTH, (g + 1) * SSM_GROUP_WIDTH)
        xw = (xs_b[:, gw] * w_end[:, gw]).astype(BF16)
        d_s = lax.dot_general(bm[:, gs], xw, (((0,), (0,)), ((), ())),
                              preferred_element_type=F32)
        state_ref[g] = s_g * chunk_decay[:, gw] + d_s
        tick()
        y_g = (jnp.concatenate(y_pairs, axis=1) + dexp[:, gw] * xs_b[:, gw]) * gate[:, gw]
        out.append(_rmsnorm(y_g, gnw[:, gw]).astype(BF16))
    return jnp.concatenate(out, axis=1)


def _ssd_kernel(x_ref, xnext_ref, nw_ref, wz_ref, wx_ref, wdt_ref, cw_ref, cb_ref, dtb_ref, alog_ref, dexp_ref,
                gnw_ref, e2_ref, pb_ref, out_ref, dtbuf_ref, xbuf_ref, zbuf_ref, tail_ref, state_ref, *, tl):
    Q = SSD_Q
    R = MXU_ROWS
    chunks = [slice(c * Q, (c + 1) * Q) for c in range(R // Q)]

    def projection(rows_ref, store):
        xn = _rmsnorm(rows_ref[...], nw_ref[...]).astype(BF16)
        res = {"x": [], "z": []}

        def dt_piece():
            dt_in = _dot(xn, wdt_ref[...]) + dtb_ref[...]
            if store:
                dtbuf_ref[...] = dt_in
            else:
                res["decays"] = [_ssd_decay(dt_in[rows], alog_ref[...], e2_ref[...]) for rows in chunks]

        def col_piece(key, w_ref, buf_ref, j):
            cols = slice(j * PROJ_PIECE, (j + 1) * PROJ_PIECE)

            def run():
                if store:
                    buf_ref[:, cols] = _dot(xn, w_ref[:, cols])
                else:
                    res[key].append(_dot(xn, w_ref[:, cols]))
            return run

        todo = [dt_piece]
        todo += [col_piece("x", wx_ref, xbuf_ref, j) for j in range(SSM_XBC // PROJ_PIECE)]
        todo += [col_piece("z", wz_ref, zbuf_ref, j) for j in range(SSM_INNER // PROJ_PIECE)]
        return todo, res

    @pl.when((pl.program_id(0) == 0) & (pl.program_id(1) == 0))
    def _():
        for run in projection(x_ref.at[0:R, :], True)[0]:
            run()

    @pl.when(pl.program_id(1) == 0)
    def _():
        tail_ref[...] = jnp.zeros_like(tail_ref)
        state_ref[...] = jnp.zeros_like(state_ref)

    def process(i, res, tail, tick):
        xraw = jnp.concatenate(res["x"], axis=1)
        z = jnp.concatenate(res["z"], axis=1)
        gate = jnp.concatenate([_silu(z[:, j * LANES:(j + 1) * LANES]).astype(BF16)
                                for j in range(SSM_INNER // LANES)], axis=1)
        xs_b, bm, cm = [], [], []
        n_x = SSM_INNER // LANES
        for j in list(range(n_x, SSM_XBC // LANES)) + list(range(n_x)):
            cols = slice(j * LANES, (j + 1) * LANES)
            blk = _silu(_causal_conv(tail[:, cols], xraw[:, cols], cw_ref[:, cols], cb_ref[:, cols]))
            if j < SSM_INNER // LANES:
                xs_b.append(blk.astype(BF16))
            elif j < (SSM_INNER + SSM_BC) // LANES:
                bm.append(blk.astype(BF16))
            else:
                cm.append(blk)
        xs_b, bm, cm = (jnp.concatenate(v, axis=1) for v in (xs_b, bm, cm))
        cbs = [[lax.dot_general(cm[rows, g * SSM_STATE:(g + 1) * SSM_STATE].astype(BF16),
                                bm[rows, g * SSM_STATE:(g + 1) * SSM_STATE], (((1,), (1,)), ((), ())),
                                preferred_element_type=F32) for g in range(SSM_GROUPS)]
               for rows in chunks]
        yn = [_ssd_scan(dec, cb, xs_b[rows], bm[rows], cm[rows], gate[rows], dexp_ref[...], gnw_ref[...],
                        state_ref, tick) for dec, cb, rows in zip(res["decays"], cbs, chunks)]
        out_ref[i * R:(i + 1) * R, :] = _dot(jnp.concatenate(yn, axis=0), pb_ref[...]).astype(BF16)
        return xraw[R - SUBLANES:R]

    n_blocks = tl // R
    tail = tail_ref[...]
    res = {"x": [xbuf_ref[...]], "z": [zbuf_ref[...]],
           "decays": [_ssd_decay(dtbuf_ref[rows, :], alog_ref[...], e2_ref[...]) for rows in chunks]}
    for i in range(n_blocks):
        if i + 1 < n_blocks:
            todo, nxt = projection(x_ref.at[(i + 1) * R:(i + 2) * R, :], False)
        else:
            todo, nxt = projection(xnext_ref, True)

        def tick():
            if todo:
                todo.pop(0)()

        for _ in range(SSD_FRONT_PIECES):
            tick()
        tail = process(i, res, tail, tick)
        while todo:
            tick()
        res = nxt
    tail_ref[...] = tail


def _ssd_branch(x2, norm_w, w_z, w_xbc, w_dt, conv_w, conv_b, dt_bias, a_log, d_exp, gnorm_w, e2, w_pb,
                B, S, tl):
    nt = S // tl
    per_tile = tl // MXU_ROWS
    last_block = B * S // MXU_ROWS - 1
    return pl.pallas_call(
        functools.partial(_ssd_kernel, tl=tl),
        grid=(B, nt),
        in_specs=[
            pl.BlockSpec((tl, D_MODEL), lambda b, t: (b * nt + t, 0)),
            pl.BlockSpec((MXU_ROWS, D_MODEL), lambda b, t: (jnp.minimum((b * nt + t + 1) * per_tile, last_block), 0)),
            _resident((1, D_MODEL)),
            _resident((D_MODEL, SSM_INNER)),
            _resident((D_MODEL, SSM_XBC)),
            _resident((D_MODEL, LANES)),
            _resident((SSM_CONV, SSM_XBC)),
            _resident((1, SSM_XBC)),
            _resident((1, LANES)),
            _resident((1, LANES)),
            _resident((1, SSM_INNER)),
            _resident((1, SSM_INNER)),
            _resident((2 * LANES, SSM_INNER)),
            _resident((SSM_INNER, D_MODEL)),
        ],
        out_specs=pl.BlockSpec((tl, D_MODEL), lambda b, t: (b * nt + t, 0)),
        out_shape=jax.ShapeDtypeStruct((B * S, D_MODEL), BF16),
        scratch_shapes=[
            pltpu.VMEM((MXU_ROWS, LANES), F32),
            pltpu.VMEM((MXU_ROWS, SSM_XBC), F32),
            pltpu.VMEM((MXU_ROWS, SSM_INNER), F32),
            pltpu.VMEM((SUBLANES, SSM_XBC), F32),
            pltpu.VMEM((SSM_GROUPS, SSM_STATE, SSM_GROUP_WIDTH), F32),
        ],
        compiler_params=_params(2),
        name="ssd_branch",
    )(x2, x2, norm_w, w_z, w_xbc, w_dt, conv_w, conv_b, dt_bias, a_log, d_exp, gnorm_w, e2, w_pb)


def _merge_kernel(x_ref, nw_ref, wg_ref, gb_ref, ya_ref, ybp_ref, pa_ref, wo_ref, h_ref, *, tm):
    for r in range(tm // MXU_ROWS):
        rows = slice(r * MXU_ROWS, (r + 1) * MXU_ROWS)
        x = x_ref[rows, :]
        xn = _rmsnorm(x, nw_ref[...]).astype(BF16)
        g = _dot(xn, wg_ref[...]) + gb_ref[...]
        m = (_sigmoid(g[:, :D_MODEL]) * _dot(ya_ref[rows, :], pa_ref[...])
             + _sigmoid(g[:, D_MODEL:]) * ybp_ref[rows, :].astype(F32))
        h_ref[rows, :] = x + _dot(m.astype(BF16), wo_ref[...])


def _merge(x2, norm_w, w_g, gate_bias, ya, ybp, w_pa, w_o, tm):
    T = x2.shape[0]
    return pl.pallas_call(
        functools.partial(_merge_kernel, tm=tm),
        grid=(T // tm,),
        in_specs=[
            pl.BlockSpec((tm, D_MODEL), lambda i: (i, 0)),
            _resident((1, D_MODEL)),
            _resident((D_MODEL, 2 * D_MODEL)),
            _resident((1, 2 * D_MODEL)),
            pl.BlockSpec((tm, GMLP_WIDTH), lambda i: (i, 0)),
            pl.BlockSpec((tm, D_MODEL), lambda i: (i, 0)),
            _resident((GMLP_WIDTH, D_MODEL)),
            _resident((D_MODEL, D_MODEL)),
        ],
        out_specs=pl.BlockSpec((tm, D_MODEL), lambda i: (i, 0)),
        out_shape=jax.ShapeDtypeStruct((T, D_MODEL), F32),
        compiler_params=_params(1),
        name="merge",
    )(x2, norm_w, w_g, gate_bias, ya, ybp, w_pa, w_o)


def _ffn_kernel(h_ref, nw_ref, wup_ref, cw_ref, cb_ref, wdn_ref, fw_ref, out_ref, tail_ref, *, tm):
    W = FFN_COLS

    @pl.when(pl.program_id(1) == 0)
    def _():
        tail_ref[...] = jnp.zeros_like(tail_ref)

    h = h_ref[...]
    hn = _rmsnorm(h, nw_ref[...]).astype(BF16)
    tail = tail_ref[...]

    def up_proj(j):
        ups = []
        for base in (0, D_FF):
            cols = slice(base + j * W, base + (j + 1) * W)
            ups.append((cols, _dot(hn, wup_ref[:, cols])))
        return ups

    n_stage = D_FF // W
    acc_out = h
    ups = [up_proj(j) for j in range(FFN_LOOKAHEAD)]
    for j in range(n_stage):
        if j + FFN_LOOKAHEAD < n_stage:
            ups.append(up_proj(j + FFN_LOOKAHEAD))
        conv = []
        for cols, up in ups.pop(0):
            conv.append(_causal_conv(tail[:, cols], up, cw_ref[:, cols], cb_ref[:, cols]))
            tail_ref[:, cols] = up[tm - SUBLANES:tm]
        act = (_silu(conv[0]) * conv[1]).astype(BF16)
        acc_out = acc_out + _dot(act, wdn_ref[j * W:(j + 1) * W, :])
    out_ref[...] = _rmsnorm(acc_out, fw_ref[...])


def _ffn(h1, norm_w, w_up, conv_w, conv_b, w_dn, final_w, B, S, tm):
    nt = S // tm
    return pl.pallas_call(
        functools.partial(_ffn_kernel, tm=tm),
        grid=(B, nt),
        in_specs=[
            pl.BlockSpec((tm, D_MODEL), lambda b, t: (b * nt + t, 0)),
            _resident((1, D_MODEL)),
            _resident((D_MODEL, 2 * D_FF)),
            _resident((FFN_CONV, 2 * D_FF)),
            _resident((1, 2 * D_FF)),
            _resident((D_FF, D_MODEL)),
            _resident((1, D_MODEL)),
        ],
        out_specs=pl.BlockSpec((tm, D_MODEL), lambda b, t: (b * nt + t, 0)),
        out_shape=jax.ShapeDtypeStruct((B * S, D_MODEL), F32),
        scratch_shapes=[pltpu.VMEM((SUBLANES, 2 * D_FF), F32)],
        compiler_params=_params(2),
        name="ffn",
    )(h1, norm_w, w_up, conv_w, conv_b, w_dn, final_w)


def _pad_lanes(v):
    return jnp.pad(v.astype(F32), (0, LANES - v.shape[0])).reshape(1, LANES)


def _tile(n, want):
    t = min(n, want)
    assert n % t == 0, (n, want)
    return t


def kernel(x, mix_norm_w, w_in, gate_bias, gmlp_ln_w, gmlp_ln_b, gmlp_ws, gmlp_bs,
           ssm_conv_w, ssm_conv_b, ssm_dt_bias, ssm_a_log, ssm_d, ssm_norm_w,
           w_proj_a, w_proj_b, w_out, ffn_norm_w, ffn_w_up, ffn_conv_w, ffn_conv_b,
           ffn_w_down, final_norm_w):
    B, S, _ = x.shape
    T = B * S
    assert S % MXU_ROWS == 0
    assert w_in.shape[0] == 1, "single-layer trunk: the final norm is fused into the FFN kernel"
    l = 0
    head_of_col = jnp.arange(SSM_INNER)[None, :] // SSM_HEAD_DIM
    expand = (jnp.arange(LANES)[:, None] == head_of_col).astype(BF16)
    expand2 = jnp.concatenate([expand, expand], axis=0)

    x2 = x.reshape(T, D_MODEL)
    w_in_b = w_in[l].astype(BF16)
    mix_w = mix_norm_w[l].reshape(1, D_MODEL)
    w_dt = jnp.pad(w_in_b[:, COL_DT:], ((0, 0), (0, LANES - SSM_HEADS)))
    bs_full = jnp.repeat(gmlp_bs[l].T, GMLP_GDIM, axis=1)

    ya = _gmlp_branch(x2, mix_w, w_in_b[:, COL_ZA:COL_Z], gmlp_ln_w[l].reshape(1, GMLP_WIDTH),
                      gmlp_ln_b[l].reshape(1, GMLP_WIDTH), gmlp_ws[l], bs_full, _tile(T, 512))
    ybp = _ssd_branch(x2, mix_w, w_in_b[:, COL_Z:COL_XBC], w_in_b[:, COL_XBC:COL_DT], w_dt,
                      ssm_conv_w[l], ssm_conv_b[l].reshape(1, SSM_XBC),
                      _pad_lanes(ssm_dt_bias[l]), _pad_lanes(ssm_a_log[l]),
                      jnp.repeat(ssm_d[l].astype(F32), SSM_HEAD_DIM).reshape(1, SSM_INNER),
                      ssm_norm_w[l].reshape(1, SSM_INNER), expand2, w_proj_b[l].astype(BF16),
                      B, S, _tile(S, 512))
    h1 = _merge(x2, mix_w, w_in_b[:, COL_GATES:COL_ZA], gate_bias[l].reshape(1, 2 * D_MODEL), ya, ybp,
                w_proj_a[l].astype(BF16), w_out[l].astype(BF16), _tile(T, 512))
    out = _ffn(h1, ffn_norm_w[l].reshape(1, D_MODEL), ffn_w_up[l].astype(BF16), ffn_conv_w[l],
               ffn_conv_b[l].reshape(1, 2 * D_FF), ffn_w_down[l].astype(BF16),
               final_norm_w.reshape(1, D_MODEL), B, S, _tile(S, 256))
    return out.reshape(B, S, D_MODEL)
```

```python
import functools
import math

import jax
import jax.numpy as jnp
from jax import lax
from jax.experimental import pallas as pl
from jax.experimental.pallas import tpu as pltpu

F32 = jnp.float32
BF16 = jnp.bfloat16

D_MODEL = 1024
EPS = 1e-5
CHUNK = 64
GMLP_BLOCK = 128
GMLP_WIDTH = 1024
GMLP_GROUPS = 8
GMLP_GDIM = GMLP_WIDTH // GMLP_GROUPS
SSM_INNER = 2 * D_MODEL
SSM_HEAD_DIM = 64
SSM_HEADS = SSM_INNER // SSM_HEAD_DIM
SSM_GROUPS = 4
SSM_STATE = 128
SSM_CONV = 4
SSM_GROUP_WIDTH = SSM_INNER // SSM_GROUPS
SSM_BC = SSM_GROUPS * SSM_STATE
SSM_XBC = SSM_INNER + 2 * SSM_BC
D_FF = 2816
FFN_CONV = 3
COL_GATES = 0
COL_ZA = 2 * D_MODEL
COL_Z = COL_ZA + 2 * GMLP_WIDTH
COL_XBC = COL_Z + SSM_INNER
COL_DT = COL_XBC + SSM_XBC

LANES = 128
SUBLANES = 8
SSD_Q = 128
MXU_ROWS = 256
PROJ_PIECE = 512
SSD_FRONT_PIECES = 5
GMLP_LOOKAHEAD = 2
FFN_COLS = 256
FFN_LOOKAHEAD = 3
VMEM_LIMIT = 56 * 1024 * 1024
LOG2E = math.log2(math.e)


def _dot(a, b):
    return jnp.dot(a, b, preferred_element_type=F32)


def _sigmoid(v):
    return 1.0 / (1.0 + jnp.exp2(v * (-LOG2E)))


def _silu(v):
    return v * _sigmoid(v)


def _causal_conv(prev_tail, cur, w, b):
    n = cur.shape[0]
    taps = w.shape[0]
    padded = jnp.concatenate([prev_tail, cur], axis=0)
    body = slice(SUBLANES, SUBLANES + n)
    if taps == 4:
        back1 = pltpu.roll(padded, 1, axis=0)
        near = w[3:4] * cur + w[2:3] * back1[body]
        far = w[1:2] * padded + w[0:1] * back1
        return b + near + pltpu.roll(far, 2, axis=0)[body]
    acc = b + w[taps - 1:taps] * cur
    for k in range(1, taps):
        shifted = pltpu.roll(padded, k, axis=0)[body]
        acc = acc + w[taps - 1 - k:taps - k] * shifted
    return acc


def _rmsnorm(v, w):
    ms = jnp.mean(v * v, axis=-1, keepdims=True)
    return v * lax.rsqrt(ms + EPS) * w


def _split3(v):
    hi = v.astype(BF16)
    r = v - hi.astype(F32)
    mid = r.astype(BF16)
    lo = (r - mid.astype(F32)).astype(BF16)
    return hi, mid, lo


def _hi_lo(v):
    hi = v.astype(BF16)
    lo = (v - hi.astype(F32)).astype(BF16)
    return jnp.concatenate([hi, lo], axis=1)


def _resident(shape):
    zeros = (0,) * len(shape)
    return pl.BlockSpec(shape, lambda *_: zeros, pipeline_mode=pl.Buffered(1))


def _params(n_axes):
    return pltpu.CompilerParams(dimension_semantics=("arbitrary",) * n_axes,
                                vmem_limit_bytes=VMEM_LIMIT)


def _gmlp_kernel(x_ref, nw_ref, wza_ref, lnw_ref, lnb_ref, ws_ref, bs_ref, out_ref, wsm_ref, *, tm):
    @pl.when(pl.program_id(0) == 0)
    def _():
        t = lax.broadcasted_iota(jnp.int32, (GMLP_BLOCK, GMLP_BLOCK), 0) // CHUNK
        s = lax.broadcasted_iota(jnp.int32, (GMLP_BLOCK, GMLP_BLOCK), 1) // CHUNK
        for g in range(GMLP_GROUPS):
            wsm_ref[g] = jnp.where(s <= t, ws_ref[g], 0.0).astype(BF16)

    xn = _rmsnorm(x_ref[...], nw_ref[...]).astype(BF16)
    PAIR = 2 * GMLP_GDIM

    def proj(r, gp):
        rows = slice(r * MXU_ROWS, (r + 1) * MXU_ROWS)
        return (_dot(xn[rows], wza_ref[:, gp * PAIR:(gp + 1) * PAIR]),
                _dot(xn[rows], wza_ref[:, GMLP_WIDTH + gp * PAIR:GMLP_WIDTH + (gp + 1) * PAIR]))

    stages = [(r, gp) for r in range(tm // MXU_ROWS) for gp in range(GMLP_GROUPS // 2)]
    pending = [proj(*st) for st in stages[:GMLP_LOOKAHEAD]]
    for i, (r, gp) in enumerate(stages):
        if i + GMLP_LOOKAHEAD < len(stages):
            pending.append(proj(*stages[i + GMLP_LOOKAHEAD]))
        zu, zv = pending.pop(0)
        for n in range(MXU_ROWS // GMLP_BLOCK):
            rows = slice(n * GMLP_BLOCK, (n + 1) * GMLP_BLOCK)
            orow = slice(r * MXU_ROWS + n * GMLP_BLOCK, r * MXU_ROWS + (n + 1) * GMLP_BLOCK)
            for k in range(2):
                g = 2 * gp + k
                cg = slice(g * GMLP_GDIM, (g + 1) * GMLP_GDIM)
                ck = slice(k * GMLP_GDIM, (k + 1) * GMLP_GDIM)
                v = jax.nn.gelu(zv[rows, ck])
                mu = jnp.mean(v, axis=-1, keepdims=True)
                d = v - mu
                var = jnp.mean(d * d, axis=-1, keepdims=True)
                vn = (d * lax.rsqrt(var + EPS) * lnw_ref[:, cg] + lnb_ref[:, cg]).astype(BF16)
                sv = _dot(wsm_ref[g], vn) + bs_ref[:, cg]
                out_ref[orow, cg] = (jax.nn.gelu(zu[rows, ck]) * sv).astype(BF16)


def _gmlp_branch(x2, norm_w, w_za, ln_w, ln_b, w_s, bs_full, tm):
    T = x2.shape[0]
    return pl.pallas_call(
        functools.partial(_gmlp_kernel, tm=tm),
        grid=(T // tm,),
        in_specs=[
            pl.BlockSpec((tm, D_MODEL), lambda i: (i, 0)),
            _resident((1, D_MODEL)),
            _resident((D_MODEL, 2 * GMLP_WIDTH)),
            _resident((1, GMLP_WIDTH)),
            _resident((1, GMLP_WIDTH)),
            _resident((GMLP_GROUPS, GMLP_BLOCK, GMLP_BLOCK)),
            _resident((GMLP_BLOCK, GMLP_WIDTH)),
        ],
        out_specs=pl.BlockSpec((tm, GMLP_WIDTH), lambda i: (i, 0)),
        out_shape=jax.ShapeDtypeStruct((T, GMLP_WIDTH), BF16),
        scratch_shapes=[pltpu.VMEM((GMLP_GROUPS, GMLP_BLOCK, GMLP_BLOCK), BF16)],
        compiler_params=_params(1),
        name="gmlp_branch",
    )(x2, norm_w, w_za, ln_w, ln_b, w_s, bs_full)


def _ssd_decay(dt_in, alog, e2):
    Q = SSD_Q
    dt = jnp.maximum(dt_in, 0.0) + jnp.log(1.0 + jnp.exp(-jnp.abs(dt_in)))
    d_a = dt * (-jnp.exp(alog))
    row = lax.broadcasted_iota(jnp.int32, (Q, Q), 0)
    col = lax.broadcasted_iota(jnp.int32, (Q, Q), 1)
    tri = (col <= row).astype(BF16)
    hi, mid, lo = _split3(d_a)
    a2 = (_dot(tri, hi) + _dot(tri, mid) + _dot(tri, lo)) * LOG2E
    a2_last = a2[Q - 1:Q, :]
    ea = jnp.exp2(a2)
    w_end = dt * jnp.exp2(a2_last - a2)
    r2_t = a2.T - jnp.log(dt.T) * LOG2E
    wide = _dot(jnp.concatenate([_hi_lo(w_end), _hi_lo(jnp.broadcast_to(jnp.exp2(a2_last), (2 * SUBLANES, LANES)))],
                                axis=0), e2)
    return a2, ea, r2_t, wide[:Q], wide[Q:Q + 1]


def _ssd_scan(decay, cbs, xs_b, bm, cm, gate, dexp, gnw, state_ref, tick):
    Q = SSD_Q
    a2, ea, r2_t, w_end, chunk_decay = decay
    row = lax.broadcasted_iota(jnp.int32, (Q, Q), 0)
    col = lax.broadcasted_iota(jnp.int32, (Q, Q), 1)
    causal = col <= row
    lane = lax.broadcasted_iota(jnp.int32, (Q, LANES), 1)
    out = []
    for g in range(SSM_GROUPS):
        gs = slice(g * SSM_STATE, (g + 1) * SSM_STATE)
        c_g = cm[:, gs]
        s_g = state_ref[g]
        s_gb = s_g.astype(BF16)
        y_pairs = []
        for hp in range(SSM_GROUP_WIDTH // LANES):
            lanes = slice(g * SSM_GROUP_WIDTH + hp * LANES, g * SSM_GROUP_WIDTH + (hp + 1) * LANES)
            f = []
            for h in (2 * (4 * g + hp), 2 * (4 * g + hp) + 1):
                seg = a2[:, h:h + 1] - r2_t[h:h + 1, :]
                g_in = cbs[g] * jnp.exp2(jnp.where(causal, seg, -jnp.inf))
                c_off = c_g * ea[:, h:h + 1]
                f.append(jnp.concatenate([g_in, c_off], axis=1).astype(BF16))
            lhs = jnp.concatenate(f, axis=0)
            rhs = jnp.concatenate([xs_b[:, lanes], s_gb[:, hp * LANES:(hp + 1) * LANES]], axis=0)
            o = _dot(lhs, rhs)
            y_pairs.append(jnp.where(lane < SSM_HEAD_DIM, o[:Q], o[Q:]))
        gw = slice(g * SSM_GROUP_WIDTH, (g + 1) * SSM_GROUP_WIDTH)
        xw = (xs_b[:, gw] * w_end[:, gw]).astype(BF16)
        d_s = lax.dot_general(bm[:, gs], xw, (((0,), (0,)), ((), ())),
                              preferred_element_type=F32)
        state_ref[g] = s_g * chunk_decay[:, gw] + d_s
        tick()
        y_g = (jnp.concatenate(y_pairs, axis=1) + dexp[:, gw] * xs_b[:, gw]) * gate[:, gw]
        out.append(_rmsnorm(y_g, gnw[:, gw]).astype(BF16))
    return jnp.concatenate(out, axis=1)


def _ssd_kernel(x_ref, xnext_ref, nw_ref, wz_ref, wx_ref, wdt_ref, cw_ref, cb_ref, dtb_ref, alog_ref, dexp_ref,
                gnw_ref, e2_ref, pb_ref, out_ref, a2buf_ref, eabuf_ref, r2buf_ref, wendbuf_ref, cdbuf_ref,
                xbuf_ref, zbuf_ref, tail_ref, state_ref, *, tl):
    Q = SSD_Q
    R = MXU_ROWS
    chunks = [slice(c * Q, (c + 1) * Q) for c in range(R // Q)]

    def projection(rows_ref, store):
        xn = _rmsnorm(rows_ref[...], nw_ref[...]).astype(BF16)
        res = {"x": [], "z": []}

        def dt_piece():
            dt_in = _dot(xn, wdt_ref[...]) + dtb_ref[...]
            decays = [_ssd_decay(dt_in[rows], alog_ref[...], e2_ref[...]) for rows in chunks]
            if store:
                for c, (a2, ea, r2_t, w_end, chunk_decay) in enumerate(decays):
                    a2buf_ref[c] = a2
                    eabuf_ref[c] = ea
                    r2buf_ref[c] = r2_t
                    wendbuf_ref[c] = w_end
                    cdbuf_ref[c] = jnp.broadcast_to(chunk_decay, (SUBLANES, SSM_INNER))
            else:
                res["decays"] = decays

        def col_piece(key, w_ref, buf_ref, j):
            cols = slice(j * PROJ_PIECE, (j + 1) * PROJ_PIECE)

            def run():
                if store:
                    buf_ref[:, cols] = _dot(xn, w_ref[:, cols])
                else:
                    res[key].append(_dot(xn, w_ref[:, cols]))
            return run

        todo = [dt_piece]
        todo += [col_piece("x", wx_ref, xbuf_ref, j) for j in range(SSM_XBC // PROJ_PIECE)]
        todo += [col_piece("z", wz_ref, zbuf_ref, j) for j in range(SSM_INNER // PROJ_PIECE)]
        return todo, res

    @pl.when((pl.program_id(0) == 0) & (pl.program_id(1) == 0))
    def _():
        for run in projection(x_ref.at[0:R, :], True)[0]:
            run()

    @pl.when(pl.program_id(1) == 0)
    def _():
        tail_ref[...] = jnp.zeros_like(tail_ref)
        state_ref[...] = jnp.zeros_like(state_ref)

    def process(i, res, tail, tick):
        xraw = jnp.concatenate(res["x"], axis=1)
        z = jnp.concatenate(res["z"], axis=1)
        gate = jnp.concatenate([_silu(z[:, j * LANES:(j + 1) * LANES]).astype(BF16)
                                for j in range(SSM_INNER // LANES)], axis=1)
        xs_b, bm, cm = [], [], []
        n_x = SSM_INNER // LANES
        for j in list(range(n_x, SSM_XBC // LANES)) + list(range(n_x)):
            cols = slice(j * LANES, (j + 1) * LANES)
            blk = _silu(_causal_conv(tail[:, cols], xraw[:, cols], cw_ref[:, cols], cb_ref[:, cols]))
            if j < SSM_INNER // LANES:
                xs_b.append(blk.astype(BF16))
            elif j < (SSM_INNER + SSM_BC) // LANES:
                bm.append(blk.astype(BF16))
            else:
                cm.append(blk)
        xs_b, bm, cm = (jnp.concatenate(v, axis=1) for v in (xs_b, bm, cm))
        cbs = [[lax.dot_general(cm[rows, g * SSM_STATE:(g + 1) * SSM_STATE].astype(BF16),
                                bm[rows, g * SSM_STATE:(g + 1) * SSM_STATE], (((1,), (1,)), ((), ())),
                                preferred_element_type=F32) for g in range(SSM_GROUPS)]
               for rows in chunks]
        yn = [_ssd_scan(dec, cb, xs_b[rows], bm[rows], cm[rows], gate[rows], dexp_ref[...], gnw_ref[...],
                        state_ref, tick) for dec, cb, rows in zip(res["decays"], cbs, chunks)]
        return xraw[R - SUBLANES:R], jnp.concatenate(yn, axis=0)

    def out_proj(i, yn):
        out_ref[i * R:(i + 1) * R, :] = _dot(yn, pb_ref[...]).astype(BF16)

    n_blocks = tl // R
    tail = tail_ref[...]
    res = {"x": [xbuf_ref[...]], "z": [zbuf_ref[...]],
           "decays": [(a2buf_ref[c], eabuf_ref[c], r2buf_ref[c], wendbuf_ref[c], cdbuf_ref[c, 0:1, :])
                      for c in range(len(chunks))]}
    for i in range(n_blocks):
        if i + 1 < n_blocks:
            todo, nxt = projection(x_ref.at[(i + 1) * R:(i + 2) * R, :], False)
        else:
            todo, nxt = projection(xnext_ref, True)

        def tick():
            if todo:
                todo.pop(0)()

        for _ in range(SSD_FRONT_PIECES):
            tick()
        if i > 0:
            out_proj(i - 1, yn)
        tail, yn = process(i, res, tail, tick)
        while todo:
            tick()
        res = nxt
    out_proj(n_blocks - 1, yn)
    tail_ref[...] = tail


def _ssd_branch(x2, norm_w, w_z, w_xbc, w_dt, conv_w, conv_b, dt_bias, a_log, d_exp, gnorm_w, e2, w_pb,
                B, S, tl):
    nt = S // tl
    per_tile = tl // MXU_ROWS
    last_block = B * S // MXU_ROWS - 1
    return pl.pallas_call(
        functools.partial(_ssd_kernel, tl=tl),
        grid=(B, nt),
        in_specs=[
            pl.BlockSpec((tl, D_MODEL), lambda b, t: (b * nt + t, 0)),
            pl.BlockSpec((MXU_ROWS, D_MODEL), lambda b, t: (jnp.minimum((b * nt + t + 1) * per_tile, last_block), 0)),
            _resident((1, D_MODEL)),
            _resident((D_MODEL, SSM_INNER)),
            _resident((D_MODEL, SSM_XBC)),
            _resident((D_MODEL, LANES)),
            _resident((SSM_CONV, SSM_XBC)),
            _resident((1, SSM_XBC)),
            _resident((1, LANES)),
            _resident((1, LANES)),
            _resident((1, SSM_INNER)),
            _resident((1, SSM_INNER)),
            _resident((2 * LANES, SSM_INNER)),
            _resident((SSM_INNER, D_MODEL)),
        ],
        out_specs=pl.BlockSpec((tl, D_MODEL), lambda b, t: (b * nt + t, 0)),
        out_shape=jax.ShapeDtypeStruct((B * S, D_MODEL), BF16),
        scratch_shapes=[
            pltpu.VMEM((MXU_ROWS // SSD_Q, SSD_Q, LANES), F32),
            pltpu.VMEM((MXU_ROWS // SSD_Q, SSD_Q, LANES), F32),
            pltpu.VMEM((MXU_ROWS // SSD_Q, LANES, SSD_Q), F32),
            pltpu.VMEM((MXU_ROWS // SSD_Q, SSD_Q, SSM_INNER), F32),
            pltpu.VMEM((MXU_ROWS // SSD_Q, SUBLANES, SSM_INNER), F32),
            pltpu.VMEM((MXU_ROWS, SSM_XBC), F32),
            pltpu.VMEM((MXU_ROWS, SSM_INNER), F32),
            pltpu.VMEM((SUBLANES, SSM_XBC), F32),
            pltpu.VMEM((SSM_GROUPS, SSM_STATE, SSM_GROUP_WIDTH), F32),
        ],
        compiler_params=_params(2),
        name="ssd_branch",
    )(x2, x2, norm_w, w_z, w_xbc, w_dt, conv_w, conv_b, dt_bias, a_log, d_exp, gnorm_w, e2, w_pb)


def _merge_kernel(x_ref, nw_ref, wg_ref, gb_ref, ya_ref, ybp_ref, pa_ref, wo_ref, h_ref, *, tm):
    for r in range(tm // MXU_ROWS):
        rows = slice(r * MXU_ROWS, (r + 1) * MXU_ROWS)
        x = x_ref[rows, :]
        xn = _rmsnorm(x, nw_ref[...]).astype(BF16)
        g = _dot(xn, wg_ref[...]) + gb_ref[...]
        m = (_sigmoid(g[:, :D_MODEL]) * _dot(ya_ref[rows, :], pa_ref[...])
             + _sigmoid(g[:, D_MODEL:]) * ybp_ref[rows, :].astype(F32))
        h_ref[rows, :] = x + _dot(m.astype(BF16), wo_ref[...])


def _merge(x2, norm_w, w_g, gate_bias, ya, ybp, w_pa, w_o, tm):
    T = x2.shape[0]
    return pl.pallas_call(
        functools.partial(_merge_kernel, tm=tm),
        grid=(T // tm,),
        in_specs=[
            pl.BlockSpec((tm, D_MODEL), lambda i: (i, 0)),
            _resident((1, D_MODEL)),
            _resident((D_MODEL, 2 * D_MODEL)),
            _resident((1, 2 * D_MODEL)),
            pl.BlockSpec((tm, GMLP_WIDTH), lambda i: (i, 0)),
            pl.BlockSpec((tm, D_MODEL), lambda i: (i, 0)),
            _resident((GMLP_WIDTH, D_MODEL)),
            _resident((D_MODEL, D_MODEL)),
        ],
        out_specs=pl.BlockSpec((tm, D_MODEL), lambda i: (i, 0)),
        out_shape=jax.ShapeDtypeStruct((T, D_MODEL), F32),
        compiler_params=_params(1),
        name="merge",
    )(x2, norm_w, w_g, gate_bias, ya, ybp, w_pa, w_o)


def _ffn_kernel(h_ref, nw_ref, wup_ref, cw_ref, cb_ref, wdn_ref, fw_ref, out_ref, tail_ref, *, tm):
    W = FFN_COLS

    @pl.when(pl.program_id(1) == 0)
    def _():
        tail_ref[...] = jnp.zeros_like(tail_ref)

    h = h_ref[...]
    hn = _rmsnorm(h, nw_ref[...]).astype(BF16)
    tail = tail_ref[...]

    def up_proj(j):
        ups = []
        for base in (0, D_FF):
            cols = slice(base + j * W, base + (j + 1) * W)
            ups.append((cols, _dot(hn, wup_ref[:, cols])))
        return ups

    n_stage = D_FF // W
    acc_out = h
    ups = [up_proj(j) for j in range(FFN_LOOKAHEAD)]
    for j in range(n_stage):
        if j + FFN_LOOKAHEAD < n_stage:
            ups.append(up_proj(j + FFN_LOOKAHEAD))
        conv = []
        for cols, up in ups.pop(0):
            conv.append(_causal_conv(tail[:, cols], up, cw_ref[:, cols], cb_ref[:, cols]))
            tail_ref[:, cols] = up[tm - SUBLANES:tm]
        act = (_silu(conv[0]) * conv[1]).astype(BF16)
        acc_out = acc_out + _dot(act, wdn_ref[j * W:(j + 1) * W, :])
    out_ref[...] = _rmsnorm(acc_out, fw_ref[...])


def _ffn(h1, norm_w, w_up, conv_w, conv_b, w_dn, final_w, B, S, tm):
    nt = S // tm
    return pl.pallas_call(
        functools.partial(_ffn_kernel, tm=tm),
        grid=(B, nt),
        in_specs=[
            pl.BlockSpec((tm, D_MODEL), lambda b, t: (b * nt + t, 0)),
            _resident((1, D_MODEL)),
            _resident((D_MODEL, 2 * D_FF)),
            _resident((FFN_CONV, 2 * D_FF)),
            _resident((1, 2 * D_FF)),
            _resident((D_FF, D_MODEL)),
            _resident((1, D_MODEL)),
        ],
        out_specs=pl.BlockSpec((tm, D_MODEL), lambda b, t: (b * nt + t, 0)),
        out_shape=jax.ShapeDtypeStruct((B * S, D_MODEL), F32),
        scratch_shapes=[pltpu.VMEM((SUBLANES, 2 * D_FF), F32)],
        compiler_params=_params(2),
        name="ffn",
    )(h1, norm_w, w_up, conv_w, conv_b, w_dn, final_w)


def _pad_lanes(v):
    return jnp.pad(v.astype(F32), (0, LANES - v.shape[0])).reshape(1, LANES)


def _tile(n, want):
    t = min(n, want)
    assert n % t == 0, (n, want)
    return t


def kernel(x, mix_norm_w, w_in, gate_bias, gmlp_ln_w, gmlp_ln_b, gmlp_ws, gmlp_bs,
           ssm_conv_w, ssm_conv_b, ssm_dt_bias, ssm_a_log, ssm_d, ssm_norm_w,
           w_proj_a, w_proj_b, w_out, ffn_norm_w, ffn_w_up, ffn_conv_w, ffn_conv_b,
           ffn_w_down, final_norm_w):
    B, S, _ = x.shape
    T = B * S
    assert S % MXU_ROWS == 0
    assert w_in.shape[0] == 1, "single-layer trunk: the final norm is fused into the FFN kernel"
    l = 0
    head_of_col = jnp.arange(SSM_INNER)[None, :] // SSM_HEAD_DIM
    expand = (jnp.arange(LANES)[:, None] == head_of_col).astype(BF16)
    expand2 = jnp.concatenate([expand, expand], axis=0)

    x2 = x.reshape(T, D_MODEL)
    w_in_b = w_in[l].astype(BF16)
    mix_w = mix_norm_w[l].reshape(1, D_MODEL)
    w_dt = jnp.pad(w_in_b[:, COL_DT:], ((0, 0), (0, LANES - SSM_HEADS)))
    bs_full = jnp.repeat(gmlp_bs[l].T, GMLP_GDIM, axis=1)

    ya = _gmlp_branch(x2, mix_w, w_in_b[:, COL_ZA:COL_Z], gmlp_ln_w[l].reshape(1, GMLP_WIDTH),
                      gmlp_ln_b[l].reshape(1, GMLP_WIDTH), gmlp_ws[l], bs_full, _tile(T, 512))
    ybp = _ssd_branch(x2, mix_w, w_in_b[:, COL_Z:COL_XBC], w_in_b[:, COL_XBC:COL_DT], w_dt,
                      ssm_conv_w[l], ssm_conv_b[l].reshape(1, SSM_XBC),
                      _pad_lanes(ssm_dt_bias[l]), _pad_lanes(ssm_a_log[l]),
                      jnp.repeat(ssm_d[l].astype(F32), SSM_HEAD_DIM).reshape(1, SSM_INNER),
                      ssm_norm_w[l].reshape(1, SSM_INNER), expand2, w_proj_b[l].astype(BF16),
                      B, S, _tile(S, 512))
    h1 = _merge(x2, mix_w, w_in_b[:, COL_GATES:COL_ZA], gate_bias[l].reshape(1, 2 * D_MODEL), ya, ybp,
                w_proj_a[l].astype(BF16), w_out[l].astype(BF16), _tile(T, 512))
    out = _ffn(h1, ffn_norm_w[l].reshape(1, D_MODEL), ffn_w_up[l].astype(BF16), ffn_conv_w[l],
               ffn_conv_b[l].reshape(1, 2 * D_FF), ffn_w_down[l].astype(BF16),
               final_norm_w.reshape(1, D_MODEL), B, S, _tile(S, 256))
    return out.reshape(B, S, D_MODEL)
```

```python
import functools
import math

import jax
import jax.numpy as jnp
from jax import lax
from jax.experimental import pallas as pl
from jax.experimental.pallas import tpu as pltpu

F32 = jnp.float32
BF16 = jnp.bfloat16

D_MODEL = 1024
EPS = 1e-5
CHUNK = 64
GMLP_BLOCK = 128
GMLP_WIDTH = 1024
GMLP_GROUPS = 8
GMLP_GDIM = GMLP_WIDTH // GMLP_GROUPS
SSM_INNER = 2 * D_MODEL
SSM_HEAD_DIM = 64
SSM_HEADS = SSM_INNER // SSM_HEAD_DIM
SSM_GROUPS = 4
SSM_STATE = 128
SSM_CONV = 4
SSM_GROUP_WIDTH = SSM_INNER // SSM_GROUPS
SSM_BC = SSM_GROUPS * SSM_STATE
SSM_XBC = SSM_INNER + 2 * SSM_BC
D_FF = 2816
FFN_CONV = 3
COL_GATES = 0
COL_ZA = 2 * D_MODEL
COL_Z = COL_ZA + 2 * GMLP_WIDTH
COL_XBC = COL_Z + SSM_INNER
COL_DT = COL_XBC + SSM_XBC

LANES = 128
SUBLANES = 8
SSD_Q = 128
MXU_ROWS = 256
PROJ_PIECE = 512
SSD_FRONT_PIECES = 5
GMLP_LOOKAHEAD = 2
FFN_COLS = 256
FFN_LOOKAHEAD = 3
VMEM_LIMIT = 56 * 1024 * 1024
LOG2E = math.log2(math.e)


def _dot(a, b):
    return jnp.dot(a, b, preferred_element_type=F32)


def _sigmoid(v):
    return 1.0 / (1.0 + jnp.exp2(v * (-LOG2E)))


def _silu(v):
    return v * _sigmoid(v)


def _causal_conv(prev_tail, cur, w, b):
    n = cur.shape[0]
    taps = w.shape[0]
    padded = jnp.concatenate([prev_tail, cur], axis=0)
    body = slice(SUBLANES, SUBLANES + n)
    if taps == 4:
        back1 = pltpu.roll(padded, 1, axis=0)
        near = w[3:4] * cur + w[2:3] * back1[body]
        far = w[1:2] * padded + w[0:1] * back1
        return b + near + pltpu.roll(far, 2, axis=0)[body]
    acc = b + w[taps - 1:taps] * cur
    for k in range(1, taps):
        shifted = pltpu.roll(padded, k, axis=0)[body]
        acc = acc + w[taps - 1 - k:taps - k] * shifted
    return acc


def _rmsnorm(v, w):
    ms = jnp.mean(v * v, axis=-1, keepdims=True)
    return v * lax.rsqrt(ms + EPS) * w


def _split3(v):
    hi = v.astype(BF16)
    r = v - hi.astype(F32)
    mid = r.astype(BF16)
    lo = (r - mid.astype(F32)).astype(BF16)
    return hi, mid, lo


def _hi_lo(v):
    hi = v.astype(BF16)
    lo = (v - hi.astype(F32)).astype(BF16)
    return jnp.concatenate([hi, lo], axis=1)


def _resident(shape):
    zeros = (0,) * len(shape)
    return pl.BlockSpec(shape, lambda *_: zeros, pipeline_mode=pl.Buffered(1))


def _params(n_axes):
    return pltpu.CompilerParams(dimension_semantics=("arbitrary",) * n_axes,
                                vmem_limit_bytes=VMEM_LIMIT)


def _mix_kernel(x_ref, ybp_ref, nw_ref, wza_ref, lnw_ref, lnb_ref, ws_ref, bs_ref, wg_ref, gb_ref, pa_ref,
                wo_ref, h_ref, wsm_ref, *, tm):
    @pl.when(pl.program_id(0) == 0)
    def _():
        t = lax.broadcasted_iota(jnp.int32, (GMLP_BLOCK, GMLP_BLOCK), 0) // CHUNK
        s = lax.broadcasted_iota(jnp.int32, (GMLP_BLOCK, GMLP_BLOCK), 1) // CHUNK
        for g in range(GMLP_GROUPS):
            wsm_ref[g] = jnp.where(s <= t, ws_ref[g], 0.0).astype(BF16)

    x = x_ref[...]
    xn = _rmsnorm(x, nw_ref[...]).astype(BF16)
    PAIR = 2 * GMLP_GDIM
    n_pairs = GMLP_GROUPS // 2

    def proj(r, gp):
        rows = slice(r * MXU_ROWS, (r + 1) * MXU_ROWS)
        return (_dot(xn[rows], wza_ref[:, gp * PAIR:(gp + 1) * PAIR]),
                _dot(xn[rows], wza_ref[:, GMLP_WIDTH + gp * PAIR:GMLP_WIDTH + (gp + 1) * PAIR]))

    def merge(r, ya):
        rows = slice(r * MXU_ROWS, (r + 1) * MXU_ROWS)
        m = (_sigmoid(gates[(r, 0)]) * _dot(ya, pa_ref[...])
             + _sigmoid(gates[(r, 1)]) * ybp_ref[rows, :].astype(F32))
        h_ref[rows, :] = x[rows] + _dot(m.astype(BF16), wo_ref[...])

    gates = {}

    def gate_proj(r, half):
        rows = slice(r * MXU_ROWS, (r + 1) * MXU_ROWS)
        cols = slice(half * D_MODEL, (half + 1) * D_MODEL)
        gates[(r, half)] = _dot(xn[rows], wg_ref[:, cols]) + gb_ref[:, cols]

    stages = [(r, gp) for r in range(tm // MXU_ROWS) for gp in range(n_pairs)]
    pending = [proj(*st) for st in stages[:GMLP_LOOKAHEAD]]
    ya_cols = []
    for i, (r, gp) in enumerate(stages):
        if i + GMLP_LOOKAHEAD < len(stages):
            pending.append(proj(*stages[i + GMLP_LOOKAHEAD]))
        zu, zv = pending.pop(0)
        for k in range(2):
            g = 2 * gp + k
            cg = slice(g * GMLP_GDIM, (g + 1) * GMLP_GDIM)
            ck = slice(k * GMLP_GDIM, (k + 1) * GMLP_GDIM)
            blocks = []
            for n in range(MXU_ROWS // GMLP_BLOCK):
                rows = slice(n * GMLP_BLOCK, (n + 1) * GMLP_BLOCK)
                v = jax.nn.gelu(zv[rows, ck])
                mu = jnp.mean(v, axis=-1, keepdims=True)
                d = v - mu
                var = jnp.mean(d * d, axis=-1, keepdims=True)
                vn = (d * lax.rsqrt(var + EPS) * lnw_ref[:, cg] + lnb_ref[:, cg]).astype(BF16)
                sv = _dot(wsm_ref[g], vn) + bs_ref[:, cg]
                blocks.append((jax.nn.gelu(zu[rows, ck]) * sv).astype(BF16))
            ya_cols.append(jnp.concatenate(blocks, axis=0))
        if gp in (1, 2):
            gate_proj(r, gp - 1)
        if gp == n_pairs - 1:
            merge(r, jnp.concatenate(ya_cols, axis=1))
            ya_cols = []


def _mix(x2, ybp, norm_w, w_za, ln_w, ln_b, w_s, bs_full, w_g, gate_bias, w_pa, w_o, tm):
    T = x2.shape[0]
    return pl.pallas_call(
        functools.partial(_mix_kernel, tm=tm),
        grid=(T // tm,),
        in_specs=[
            pl.BlockSpec((tm, D_MODEL), lambda i: (i, 0)),
            pl.BlockSpec((tm, D_MODEL), lambda i: (i, 0)),
            _resident((1, D_MODEL)),
            _resident((D_MODEL, 2 * GMLP_WIDTH)),
            _resident((1, GMLP_WIDTH)),
            _resident((1, GMLP_WIDTH)),
            _resident((GMLP_GROUPS, GMLP_BLOCK, GMLP_BLOCK)),
            _resident((GMLP_BLOCK, GMLP_WIDTH)),
            _resident((D_MODEL, 2 * D_MODEL)),
            _resident((1, 2 * D_MODEL)),
            _resident((GMLP_WIDTH, D_MODEL)),
            _resident((D_MODEL, D_MODEL)),
        ],
        out_specs=pl.BlockSpec((tm, D_MODEL), lambda i: (i, 0)),
        out_shape=jax.ShapeDtypeStruct((T, D_MODEL), F32),
        scratch_shapes=[pltpu.VMEM((GMLP_GROUPS, GMLP_BLOCK, GMLP_BLOCK), BF16)],
        compiler_params=_params(1),
        name="gmlp_merge",
    )(x2, ybp, norm_w, w_za, ln_w, ln_b, w_s, bs_full, w_g, gate_bias, w_pa, w_o)


def _ssd_decay(dt_in, alog, e2):
    Q = SSD_Q
    dt = jnp.maximum(dt_in, 0.0) + jnp.log(1.0 + jnp.exp(-jnp.abs(dt_in)))
    d_a = dt * (-jnp.exp(alog))
    row = lax.broadcasted_iota(jnp.int32, (Q, Q), 0)
    col = lax.broadcasted_iota(jnp.int32, (Q, Q), 1)
    tri = (col <= row).astype(BF16)
    hi, mid, lo = _split3(d_a)
    a2 = (_dot(tri, hi) + _dot(tri, mid) + _dot(tri, lo)) * LOG2E
    a2_last = a2[Q - 1:Q, :]
    ea = jnp.exp2(a2)
    w_end = dt * jnp.exp2(a2_last - a2)
    r2_t = a2.T - jnp.log(dt.T) * LOG2E
    wide = _dot(jnp.concatenate([_hi_lo(w_end), _hi_lo(jnp.broadcast_to(jnp.exp2(a2_last), (2 * SUBLANES, LANES)))],
                                axis=0), e2)
    return a2, ea, r2_t, wide[:Q], wide[Q:Q + 1]


def _ssd_scan(decay, cbs, xs_b, bm, cm, gate, dexp, gnw, state_ref, tick):
    Q = SSD_Q
    a2, ea, r2_t, w_end, chunk_decay = decay
    row = lax.broadcasted_iota(jnp.int32, (Q, Q), 0)
    col = lax.broadcasted_iota(jnp.int32, (Q, Q), 1)
    causal = col <= row
    lane = lax.broadcasted_iota(jnp.int32, (Q, LANES), 1)
    out = []
    for g in range(SSM_GROUPS):
        gs = slice(g * SSM_STATE, (g + 1) * SSM_STATE)
        c_g = cm[:, gs]
        s_g = state_ref[g]
        s_gb = s_g.astype(BF16)
        y_pairs = []
        for hp in range(SSM_GROUP_WIDTH // LANES):
            lanes = slice(g * SSM_GROUP_WIDTH + hp * LANES, g * SSM_GROUP_WIDTH + (hp + 1) * LANES)
            f = []
            for h in (2 * (4 * g + hp), 2 * (4 * g + hp) + 1):
                seg = a2[:, h:h + 1] - r2_t[h:h + 1, :]
                g_in = cbs[g] * jnp.exp2(jnp.where(causal, seg, -jnp.inf))
                c_off = c_g * ea[:, h:h + 1]
                f.append(jnp.concatenate([g_in, c_off], axis=1).astype(BF16))
            lhs = jnp.concatenate(f, axis=0)
            rhs = jnp.concatenate([xs_b[:, lanes], s_gb[:, hp * LANES:(hp + 1) * LANES]], axis=0)
            o = _dot(lhs, rhs)
            y_pairs.append(jnp.where(lane < SSM_HEAD_DIM, o[:Q], o[Q:]))
        gw = slice(g * SSM_GROUP_WIDTH, (g + 1) * SSM_GROUP_WIDTH)
        xw = (xs_b[:, gw] * w_end[:, gw]).astype(BF16)
        d_s = lax.dot_general(bm[:, gs], xw, (((0,), (0,)), ((), ())),
                              preferred_element_type=F32)
        state_ref[g] = s_g * chunk_decay[:, gw] + d_s
        tick()
        y_g = (jnp.concatenate(y_pairs, axis=1) + dexp[:, gw] * xs_b[:, gw]) * gate[:, gw]
        out.append(_rmsnorm(y_g, gnw[:, gw]).astype(BF16))
    return jnp.concatenate(out, axis=1)


def _ssd_kernel(x_ref, xnext_ref, nw_ref, wz_ref, wx_ref, wdt_ref, cw_ref, cb_ref, dtb_ref, alog_ref, dexp_ref,
                gnw_ref, e2_ref, pb_ref, out_ref, a2buf_ref, eabuf_ref, r2buf_ref, wendbuf_ref, cdbuf_ref,
                xbuf_ref, zbuf_ref, tail_ref, state_ref, *, tl):
    Q = SSD_Q
    R = MXU_ROWS
    chunks = [slice(c * Q, (c + 1) * Q) for c in range(R // Q)]

    def projection(rows_ref, store):
        xn = _rmsnorm(rows_ref[...], nw_ref[...]).astype(BF16)
        res = {"x": [], "z": []}

        def dt_piece():
            dt_in = _dot(xn, wdt_ref[...]) + dtb_ref[...]
            decays = [_ssd_decay(dt_in[rows], alog_ref[...], e2_ref[...]) for rows in chunks]
            if store:
                for c, (a2, ea, r2_t, w_end, chunk_decay) in enumerate(decays):
                    a2buf_ref[c] = a2
                    eabuf_ref[c] = ea
                    r2buf_ref[c] = r2_t
                    wendbuf_ref[c] = w_end
                    cdbuf_ref[c] = jnp.broadcast_to(chunk_decay, (SUBLANES, SSM_INNER))
            else:
                res["decays"] = decays

        def col_piece(key, w_ref, buf_ref, j):
            cols = slice(j * PROJ_PIECE, (j + 1) * PROJ_PIECE)

            def run():
                if store:
                    buf_ref[:, cols] = _dot(xn, w_ref[:, cols])
                else:
                    res[key].append(_dot(xn, w_ref[:, cols]))
            return run

        todo = [dt_piece]
        todo += [col_piece("x", wx_ref, xbuf_ref, j) for j in range(SSM_XBC // PROJ_PIECE)]
        todo += [col_piece("z", wz_ref, zbuf_ref, j) for j in range(SSM_INNER // PROJ_PIECE)]
        return todo, res

    @pl.when((pl.program_id(0) == 0) & (pl.program_id(1) == 0))
    def _():
        for run in projection(x_ref.at[0:R, :], True)[0]:
            run()

    @pl.when(pl.program_id(1) == 0)
    def _():
        tail_ref[...] = jnp.zeros_like(tail_ref)
        state_ref[...] = jnp.zeros_like(state_ref)

    def process(i, res, tail, tick):
        xraw = jnp.concatenate(res["x"], axis=1)
        z = jnp.concatenate(res["z"], axis=1)
        gate = jnp.concatenate([_silu(z[:, j * LANES:(j + 1) * LANES]).astype(BF16)
                                for j in range(SSM_INNER // LANES)], axis=1)
        xs_b, bm, cm = [], [], []
        n_x = SSM_INNER // LANES
        for j in list(range(n_x, SSM_XBC // LANES)) + list(range(n_x)):
            cols = slice(j * LANES, (j + 1) * LANES)
            blk = _silu(_causal_conv(tail[:, cols], xraw[:, cols], cw_ref[:, cols], cb_ref[:, cols]))
            if j < SSM_INNER // LANES:
                xs_b.append(blk.astype(BF16))
            elif j < (SSM_INNER + SSM_BC) // LANES:
                bm.append(blk.astype(BF16))
            else:
                cm.append(blk)
        xs_b, bm, cm = (jnp.concatenate(v, axis=1) for v in (xs_b, bm, cm))
        cbs = [[lax.dot_general(cm[rows, g * SSM_STATE:(g + 1) * SSM_STATE].astype(BF16),
                                bm[rows, g * SSM_STATE:(g + 1) * SSM_STATE], (((1,), (1,)), ((), ())),
                                preferred_element_type=F32) for g in range(SSM_GROUPS)]
               for rows in chunks]
        yn = [_ssd_scan(dec, cb, xs_b[rows], bm[rows], cm[rows], gate[rows], dexp_ref[...], gnw_ref[...],
                        state_ref, tick) for dec, cb, rows in zip(res["decays"], cbs, chunks)]
        return xraw[R - SUBLANES:R], jnp.concatenate(yn, axis=0)

    def out_proj(i, yn):
        out_ref[i * R:(i + 1) * R, :] = _dot(yn, pb_ref[...]).astype(BF16)

    n_blocks = tl // R
    tail = tail_ref[...]
    res = {"x": [xbuf_ref[...]], "z": [zbuf_ref[...]],
           "decays": [(a2buf_ref[c], eabuf_ref[c], r2buf_ref[c], wendbuf_ref[c], cdbuf_ref[c, 0:1, :])
                      for c in range(len(chunks))]}
    for i in range(n_blocks):
        if i + 1 < n_blocks:
            todo, nxt = projection(x_ref.at[(i + 1) * R:(i + 2) * R, :], False)
        else:
            todo, nxt = projection(xnext_ref, True)

        def tick():
            if todo:
                todo.pop(0)()

        for _ in range(SSD_FRONT_PIECES):
            tick()
        if i > 0:
            out_proj(i - 1, yn)
        tail, yn = process(i, res, tail, tick)
        while todo:
            tick()
        res = nxt
    out_proj(n_blocks - 1, yn)
    tail_ref[...] = tail


def _ssd_branch(x2, norm_w, w_z, w_xbc, w_dt, conv_w, conv_b, dt_bias, a_log, d_exp, gnorm_w, e2, w_pb,
                B, S, tl):
    nt = S // tl
    per_tile = tl // MXU_ROWS
    last_block = B * S // MXU_ROWS - 1
    return pl.pallas_call(
        functools.partial(_ssd_kernel, tl=tl),
        grid=(B, nt),
        in_specs=[
            pl.BlockSpec((tl, D_MODEL), lambda b, t: (b * nt + t, 0)),
            pl.BlockSpec((MXU_ROWS, D_MODEL), lambda b, t: (jnp.minimum((b * nt + t + 1) * per_tile, last_block), 0)),
            _resident((1, D_MODEL)),
            _resident((D_MODEL, SSM_INNER)),
            _resident((D_MODEL, SSM_XBC)),
            _resident((D_MODEL, LANES)),
            _resident((SSM_CONV, SSM_XBC)),
            _resident((1, SSM_XBC)),
            _resident((1, LANES)),
            _resident((1, LANES)),
            _resident((1, SSM_INNER)),
            _resident((1, SSM_INNER)),
            _resident((2 * LANES, SSM_INNER)),
            _resident((SSM_INNER, D_MODEL)),
        ],
        out_specs=pl.BlockSpec((tl, D_MODEL), lambda b, t: (b * nt + t, 0)),
        out_shape=jax.ShapeDtypeStruct((B * S, D_MODEL), BF16),
        scratch_shapes=[
            pltpu.VMEM((MXU_ROWS // SSD_Q, SSD_Q, LANES), F32),
            pltpu.VMEM((MXU_ROWS // SSD_Q, SSD_Q, LANES), F32),
            pltpu.VMEM((MXU_ROWS // SSD_Q, LANES, SSD_Q), F32),
            pltpu.VMEM((MXU_ROWS // SSD_Q, SSD_Q, SSM_INNER), F32),
            pltpu.VMEM((MXU_ROWS // SSD_Q, SUBLANES, SSM_INNER), F32),
            pltpu.VMEM((MXU_ROWS, SSM_XBC), F32),
            pltpu.VMEM((MXU_ROWS, SSM_INNER), F32),
            pltpu.VMEM((SUBLANES, SSM_XBC), F32),
            pltpu.VMEM((SSM_GROUPS, SSM_STATE, SSM_GROUP_WIDTH), F32),
        ],
        compiler_params=_params(2),
        name="ssd_branch",
    )(x2, x2, norm_w, w_z, w_xbc, w_dt, conv_w, conv_b, dt_bias, a_log, d_exp, gnorm_w, e2, w_pb)


def _ffn_kernel(h_ref, nw_ref, wup_ref, cw_ref, cb_ref, wdn_ref, fw_ref, out_ref, tail_ref, *, tm):
    W = FFN_COLS

    @pl.when(pl.program_id(1) == 0)
    def _():
        tail_ref[...] = jnp.zeros_like(tail_ref)

    h = h_ref[...]
    hn = _rmsnorm(h, nw_ref[...]).astype(BF16)
    tail = tail_ref[...]

    def up_proj(j):
        ups = []
        for base in (0, D_FF):
            cols = slice(base + j * W, base + (j + 1) * W)
            ups.append((cols, _dot(hn, wup_ref[:, cols])))
        return ups

    n_stage = D_FF // W
    acc_out = h
    ups = [up_proj(j) for j in range(FFN_LOOKAHEAD)]
    for j in range(n_stage):
        if j + FFN_LOOKAHEAD < n_stage:
            ups.append(up_proj(j + FFN_LOOKAHEAD))
        conv = []
        for cols, up in ups.pop(0):
            conv.append(_causal_conv(tail[:, cols], up, cw_ref[:, cols], cb_ref[:, cols]))
            tail_ref[:, cols] = up[tm - SUBLANES:tm]
        act = (_silu(conv[0]) * conv[1]).astype(BF16)
        acc_out = acc_out + _dot(act, wdn_ref[j * W:(j + 1) * W, :])
    out_ref[...] = _rmsnorm(acc_out, fw_ref[...])


def _ffn(h1, norm_w, w_up, conv_w, conv_b, w_dn, final_w, B, S, tm):
    nt = S // tm
    return pl.pallas_call(
        functools.partial(_ffn_kernel, tm=tm),
        grid=(B, nt),
        in_specs=[
            pl.BlockSpec((tm, D_MODEL), lambda b, t: (b * nt + t, 0)),
            _resident((1, D_MODEL)),
            _resident((D_MODEL, 2 * D_FF)),
            _resident((FFN_CONV, 2 * D_FF)),
            _resident((1, 2 * D_FF)),
            _resident((D_FF, D_MODEL)),
            _resident((1, D_MODEL)),
        ],
        out_specs=pl.BlockSpec((tm, D_MODEL), lambda b, t: (b * nt + t, 0)),
        out_shape=jax.ShapeDtypeStruct((B * S, D_MODEL), F32),
        scratch_shapes=[pltpu.VMEM((SUBLANES, 2 * D_FF), F32)],
        compiler_params=_params(2),
        name="ffn",
    )(h1, norm_w, w_up, conv_w, conv_b, w_dn, final_w)


def _pad_lanes(v):
    return jnp.pad(v.astype(F32), (0, LANES - v.shape[0])).reshape(1, LANES)


def _tile(n, want):
    t = min(n, want)
    assert n % t == 0, (n, want)
    return t


def kernel(x, mix_norm_w, w_in, gate_bias, gmlp_ln_w, gmlp_ln_b, gmlp_ws, gmlp_bs,
           ssm_conv_w, ssm_conv_b, ssm_dt_bias, ssm_a_log, ssm_d, ssm_norm_w,
           w_proj_a, w_proj_b, w_out, ffn_norm_w, ffn_w_up, ffn_conv_w, ffn_conv_b,
           ffn_w_down, final_norm_w):
    B, S, _ = x.shape
    T = B * S
    assert S % MXU_ROWS == 0
    assert w_in.shape[0] == 1, "single-layer trunk: the final norm is fused into the FFN kernel"
    l = 0
    head_of_col = jnp.arange(SSM_INNER)[None, :] // SSM_HEAD_DIM
    expand = (jnp.arange(LANES)[:, None] == head_of_col).astype(BF16)
    expand2 = jnp.concatenate([expand, expand], axis=0)

    x2 = x.reshape(T, D_MODEL)
    w_in_b = w_in[l].astype(BF16)
    mix_w = mix_norm_w[l].reshape(1, D_MODEL)
    w_dt = jnp.pad(w_in_b[:, COL_DT:], ((0, 0), (0, LANES - SSM_HEADS)))
    bs_full = jnp.repeat(gmlp_bs[l].T, GMLP_GDIM, axis=1)

    ybp = _ssd_branch(x2, mix_w, w_in_b[:, COL_Z:COL_XBC], w_in_b[:, COL_XBC:COL_DT], w_dt,
                      ssm_conv_w[l], ssm_conv_b[l].reshape(1, SSM_XBC),
                      _pad_lanes(ssm_dt_bias[l]), _pad_lanes(ssm_a_log[l]),
                      jnp.repeat(ssm_d[l].astype(F32), SSM_HEAD_DIM).reshape(1, SSM_INNER),
                      ssm_norm_w[l].reshape(1, SSM_INNER), expand2, w_proj_b[l].astype(BF16),
                      B, S, _tile(S, 512))
    h1 = _mix(x2, ybp, mix_w, w_in_b[:, COL_ZA:COL_Z], gmlp_ln_w[l].reshape(1, GMLP_WIDTH),
              gmlp_ln_b[l].reshape(1, GMLP_WIDTH), gmlp_ws[l], bs_full, w_in_b[:, COL_GATES:COL_ZA],
              gate_bias[l].reshape(1, 2 * D_MODEL), w_proj_a[l].astype(BF16), w_out[l].astype(BF16),
              _tile(T, 512))
    out = _ffn(h1, ffn_norm_w[l].reshape(1, D_MODEL), ffn_w_up[l].astype(BF16), ffn_conv_w[l],
               ffn_conv_b[l].reshape(1, 2 * D_FF), ffn_w_down[l].astype(BF16),
               final_norm_w.reshape(1, D_MODEL), B, S, _tile(S, 256))
    return out.reshape(B, S, D_MODEL)
```

```python
import functools
import math

import jax
import jax.numpy as jnp
from jax import lax
from jax.experimental import pallas as pl
from jax.experimental.pallas import tpu as pltpu

F32 = jnp.float32
BF16 = jnp.bfloat16

D_MODEL = 1024
EPS = 1e-5
CHUNK = 64
GMLP_BLOCK = 128
GMLP_WIDTH = 1024
GMLP_GROUPS = 8
GMLP_GDIM = GMLP_WIDTH // GMLP_GROUPS
SSM_INNER = 2 * D_MODEL
SSM_HEAD_DIM = 64
SSM_HEADS = SSM_INNER // SSM_HEAD_DIM
SSM_GROUPS = 4
SSM_STATE = 128
SSM_CONV = 4
SSM_GROUP_WIDTH = SSM_INNER // SSM_GROUPS
SSM_BC = SSM_GROUPS * SSM_STATE
SSM_XBC = SSM_INNER + 2 * SSM_BC
D_FF = 2816
FFN_CONV = 3
COL_GATES = 0
COL_ZA = 2 * D_MODEL
COL_Z = COL_ZA + 2 * GMLP_WIDTH
COL_XBC = COL_Z + SSM_INNER
COL_DT = COL_XBC + SSM_XBC

LANES = 128
SUBLANES = 8
SSD_Q = 128
MXU_ROWS = 256
PROJ_PIECE = 512
SSD_FRONT_PIECES = 5
GMLP_LOOKAHEAD = 2
FFN_COLS = 256
FFN_LOOKAHEAD = 3
VMEM_LIMIT = 56 * 1024 * 1024
LOG2E = math.log2(math.e)


def _dot(a, b):
    return jnp.dot(a, b, preferred_element_type=F32)


def _sigmoid(v):
    return 1.0 / (1.0 + jnp.exp2(v * (-LOG2E)))


def _silu(v):
    return v * _sigmoid(v)


def _causal_conv(prev_tail, cur, w, b):
    n = cur.shape[0]
    taps = w.shape[0]
    padded = jnp.concatenate([prev_tail, cur], axis=0)
    body = slice(SUBLANES, SUBLANES + n)
    if taps == 4:
        back1 = pltpu.roll(padded, 1, axis=0)
        near = w[3:4] * cur + w[2:3] * back1[body]
        far = w[1:2] * padded + w[0:1] * back1
        return b + near + pltpu.roll(far, 2, axis=0)[body]
    acc = b + w[taps - 1:taps] * cur
    for k in range(1, taps):
        shifted = pltpu.roll(padded, k, axis=0)[body]
        acc = acc + w[taps - 1 - k:taps - k] * shifted
    return acc


def _rmsnorm(v, w):
    ms = jnp.mean(v * v, axis=-1, keepdims=True)
    return v * lax.rsqrt(ms + EPS) * w


def _split3(v):
    hi = v.astype(BF16)
    r = v - hi.astype(F32)
    mid = r.astype(BF16)
    lo = (r - mid.astype(F32)).astype(BF16)
    return hi, mid, lo


def _hi_lo(v):
    hi = v.astype(BF16)
    lo = (v - hi.astype(F32)).astype(BF16)
    return jnp.concatenate([hi, lo], axis=1)


def _resident(shape):
    zeros = (0,) * len(shape)
    return pl.BlockSpec(shape, lambda *_: zeros, pipeline_mode=pl.Buffered(1))


def _params(n_axes):
    return pltpu.CompilerParams(dimension_semantics=("arbitrary",) * n_axes,
                                vmem_limit_bytes=VMEM_LIMIT)


def _mix_kernel(x_ref, ybp_ref, nw_ref, wza_ref, lnw_ref, lnb_ref, ws_ref, bs_ref, wg_ref, gb_ref, pa_ref,
                wo_ref, h_ref, wsm_ref, *, tm):
    @pl.when(pl.program_id(0) == 0)
    def _():
        t = lax.broadcasted_iota(jnp.int32, (GMLP_BLOCK, GMLP_BLOCK), 0) // CHUNK
        s = lax.broadcasted_iota(jnp.int32, (GMLP_BLOCK, GMLP_BLOCK), 1) // CHUNK
        for g in range(GMLP_GROUPS):
            wsm_ref[g] = jnp.where(s <= t, ws_ref[g], 0.0).astype(BF16)

    x = x_ref[...]
    xn = _rmsnorm(x, nw_ref[...]).astype(BF16)
    PAIR = 2 * GMLP_GDIM
    n_pairs = GMLP_GROUPS // 2

    def proj(r, gp):
        rows = slice(r * MXU_ROWS, (r + 1) * MXU_ROWS)
        return (_dot(xn[rows], wza_ref[:, gp * PAIR:(gp + 1) * PAIR]),
                _dot(xn[rows], wza_ref[:, GMLP_WIDTH + gp * PAIR:GMLP_WIDTH + (gp + 1) * PAIR]))

    def merge(r, ya):
        rows = slice(r * MXU_ROWS, (r + 1) * MXU_ROWS)
        m = (_sigmoid(gates[(r, 0)]) * _dot(ya, pa_ref[...])
             + _sigmoid(gates[(r, 1)]) * ybp_ref[rows, :].astype(F32))
        h_ref[rows, :] = x[rows] + _dot(m.astype(BF16), wo_ref[...])

    gates = {}

    def gate_proj(r, half):
        rows = slice(r * MXU_ROWS, (r + 1) * MXU_ROWS)
        cols = slice(half * D_MODEL, (half + 1) * D_MODEL)
        gates[(r, half)] = _dot(xn[rows], wg_ref[:, cols]) + gb_ref[:, cols]

    stages = [(r, gp) for r in range(tm // MXU_ROWS) for gp in range(n_pairs)]
    pending = [proj(*st) for st in stages[:GMLP_LOOKAHEAD]]
    ya_cols = []
    for i, (r, gp) in enumerate(stages):
        if i + GMLP_LOOKAHEAD < len(stages):
            pending.append(proj(*stages[i + GMLP_LOOKAHEAD]))
        zu, zv = pending.pop(0)
        for k in range(2):
            g = 2 * gp + k
            cg = slice(g * GMLP_GDIM, (g + 1) * GMLP_GDIM)
            ck = slice(k * GMLP_GDIM, (k + 1) * GMLP_GDIM)
            blocks = []
            for n in range(MXU_ROWS // GMLP_BLOCK):
                rows = slice(n * GMLP_BLOCK, (n + 1) * GMLP_BLOCK)
                v = jax.nn.gelu(zv[rows, ck])
                mu = jnp.mean(v, axis=-1, keepdims=True)
                d = v - mu
                var = jnp.mean(d * d, axis=-1, keepdims=True)
                vn = (d * lax.rsqrt(var + EPS) * lnw_ref[:, cg] + lnb_ref[:, cg]).astype(BF16)
                sv = _dot(wsm_ref[g], vn) + bs_ref[:, cg]
                blocks.append((jax.nn.gelu(zu[rows, ck]) * sv).astype(BF16))
            ya_cols.append(jnp.concatenate(blocks, axis=0))
        if gp in (1, 2):
            gate_proj(r, gp - 1)
        if gp == n_pairs - 1:
            merge(r, jnp.concatenate(ya_cols, axis=1))
            ya_cols = []


def _mix(x2, ybp, norm_w, w_za, ln_w, ln_b, w_s, bs_full, w_g, gate_bias, w_pa, w_o, tm):
    T = x2.shape[0]
    return pl.pallas_call(
        functools.partial(_mix_kernel, tm=tm),
        grid=(T // tm,),
        in_specs=[
            pl.BlockSpec((tm, D_MODEL), lambda i: (i, 0)),
            pl.BlockSpec((tm, D_MODEL), lambda i: (i, 0)),
            _resident((1, D_MODEL)),
            _resident((D_MODEL, 2 * GMLP_WIDTH)),
            _resident((1, GMLP_WIDTH)),
            _resident((1, GMLP_WIDTH)),
            _resident((GMLP_GROUPS, GMLP_BLOCK, GMLP_BLOCK)),
            _resident((GMLP_BLOCK, GMLP_WIDTH)),
            _resident((D_MODEL, 2 * D_MODEL)),
            _resident((1, 2 * D_MODEL)),
            _resident((GMLP_WIDTH, D_MODEL)),
            _resident((D_MODEL, D_MODEL)),
        ],
        out_specs=pl.BlockSpec((tm, D_MODEL), lambda i: (i, 0)),
        out_shape=jax.ShapeDtypeStruct((T, D_MODEL), F32),
        scratch_shapes=[pltpu.VMEM((GMLP_GROUPS, GMLP_BLOCK, GMLP_BLOCK), BF16)],
        compiler_params=_params(1),
        name="gmlp_merge",
    )(x2, ybp, norm_w, w_za, ln_w, ln_b, w_s, bs_full, w_g, gate_bias, w_pa, w_o)


def _ssd_decay(dt_in, alog, e2):
    Q = SSD_Q
    dt = jnp.maximum(dt_in, 0.0) + jnp.log(1.0 + jnp.exp(-jnp.abs(dt_in)))
    d_a = dt * (-jnp.exp(alog))
    row = lax.broadcasted_iota(jnp.int32, (Q, Q), 0)
    col = lax.broadcasted_iota(jnp.int32, (Q, Q), 1)
    tri = (col <= row).astype(BF16)
    hi, mid, lo = _split3(d_a)
    a2 = (_dot(tri, hi) + _dot(tri, mid) + _dot(tri, lo)) * LOG2E
    a2_last = a2[Q - 1:Q, :]
    ea = jnp.exp2(a2)
    w_end = dt * jnp.exp2(a2_last - a2)
    r2_t = a2.T - jnp.log(dt.T) * LOG2E
    wide = _dot(jnp.concatenate([_hi_lo(w_end), _hi_lo(jnp.broadcast_to(jnp.exp2(a2_last), (2 * SUBLANES, LANES)))],
                                axis=0), e2)
    return a2, ea, r2_t, wide[:Q], wide[Q:Q + 1]


def _ssd_scan(decay, cbs, xs_b, bm, cm, gate, dexp, gnw, state_ref, tick):
    Q = SSD_Q
    a2, ea, r2_t, w_end, chunk_decay = decay
    row = lax.broadcasted_iota(jnp.int32, (Q, Q), 0)
    col = lax.broadcasted_iota(jnp.int32, (Q, Q), 1)
    causal = col <= row
    lane = lax.broadcasted_iota(jnp.int32, (Q, LANES), 1)
    out = []
    for g in range(SSM_GROUPS):
        gs = slice(g * SSM_STATE, (g + 1) * SSM_STATE)
        c_g = cm[:, gs]
        s_g = state_ref[g]
        s_gb = s_g.astype(BF16)
        y_pairs = []
        for hp in range(SSM_GROUP_WIDTH // LANES):
            lanes = slice(g * SSM_GROUP_WIDTH + hp * LANES, g * SSM_GROUP_WIDTH + (hp + 1) * LANES)
            f = []
            for h in (2 * (4 * g + hp), 2 * (4 * g + hp) + 1):
                seg = a2[:, h:h + 1] - r2_t[h:h + 1, :]
                g_in = cbs[g] * jnp.exp2(jnp.where(causal, seg, -jnp.inf))
                c_off = c_g * ea[:, h:h + 1]
                f.append(jnp.concatenate([g_in, c_off], axis=1).astype(BF16))
            lhs = jnp.concatenate(f, axis=0)
            rhs = jnp.concatenate([xs_b[:, lanes], s_gb[:, hp * LANES:(hp + 1) * LANES]], axis=0)
            o = _dot(lhs, rhs)
            y_pairs.append(jnp.where(lane < SSM_HEAD_DIM, o[:Q], o[Q:]))
        gw = slice(g * SSM_GROUP_WIDTH, (g + 1) * SSM_GROUP_WIDTH)
        xw = (xs_b[:, gw] * w_end[:, gw]).astype(BF16)
        d_s = lax.dot_general(bm[:, gs], xw, (((0,), (0,)), ((), ())),
                              preferred_element_type=F32)
        state_ref[g] = s_g * chunk_decay[:, gw] + d_s
        tick()
        y_g = (jnp.concatenate(y_pairs, axis=1) + dexp[:, gw] * xs_b[:, gw]) * gate[:, gw]
        out.append(_rmsnorm(y_g, gnw[:, gw]).astype(BF16))
    return jnp.concatenate(out, axis=1)


def _ssd_kernel(x_ref, xnext_ref, nw_ref, wz_ref, wx_ref, wdt_ref, cw_ref, cb_ref, dtb_ref, alog_ref, dexp_ref,
                gnw_ref, e2_ref, pb_ref, out_ref, a2buf_ref, eabuf_ref, r2buf_ref, wendbuf_ref, cdbuf_ref,
                xbuf_ref, zbuf_ref, tail_ref, state_ref, *, tl):
    Q = SSD_Q
    R = MXU_ROWS
    chunks = [slice(c * Q, (c + 1) * Q) for c in range(R // Q)]

    def projection(rows_ref, store):
        xn = _rmsnorm(rows_ref[...], nw_ref[...]).astype(BF16)
        res = {"x": [], "z": []}

        def dt_piece():
            dt_in = _dot(xn, wdt_ref[...]) + dtb_ref[...]
            decays = [_ssd_decay(dt_in[rows], alog_ref[...], e2_ref[...]) for rows in chunks]
            if store:
                for c, (a2, ea, r2_t, w_end, chunk_decay) in enumerate(decays):
                    a2buf_ref[c] = a2
                    eabuf_ref[c] = ea
                    r2buf_ref[c] = r2_t
                    wendbuf_ref[c] = w_end
                    cdbuf_ref[c] = jnp.broadcast_to(chunk_decay, (SUBLANES, SSM_INNER))
            else:
                res["decays"] = decays

        def col_piece(key, w_ref, buf_ref, j):
            cols = slice(j * PROJ_PIECE, (j + 1) * PROJ_PIECE)

            def run():
                if store:
                    buf_ref[:, cols] = _dot(xn, w_ref[:, cols])
                else:
                    res[key].append(_dot(xn, w_ref[:, cols]))
            return run

        todo = [dt_piece]
        todo += [col_piece("x", wx_ref, xbuf_ref, j) for j in range(SSM_XBC // PROJ_PIECE)]
        todo += [col_piece("z", wz_ref, zbuf_ref, j) for j in range(SSM_INNER // PROJ_PIECE)]
        return todo, res

    @pl.when((pl.program_id(0) == 0) & (pl.program_id(1) == 0))
    def _():
        for run in projection(x_ref.at[0:R, :], True)[0]:
            run()

    @pl.when(pl.program_id(1) == 0)
    def _():
        tail_ref[...] = jnp.zeros_like(tail_ref)
        state_ref[...] = jnp.zeros_like(state_ref)

    def process(i, res, tail, tick):
        xraw = jnp.concatenate(res["x"], axis=1)
        z = jnp.concatenate(res["z"], axis=1)
        gate = jnp.concatenate([_silu(z[:, j * LANES:(j + 1) * LANES]).astype(BF16)
                                for j in range(SSM_INNER // LANES)], axis=1)
        xs_b, bm, cm = [], [], []
        n_x = SSM_INNER // LANES
        for j in list(range(n_x, SSM_XBC // LANES)) + list(range(n_x)):
            cols = slice(j * LANES, (j + 1) * LANES)
            blk = _silu(_causal_conv(tail[:, cols], xraw[:, cols], cw_ref[:, cols], cb_ref[:, cols]))
            if j < SSM_INNER // LANES:
                xs_b.append(blk.astype(BF16))
            elif j < (SSM_INNER + SSM_BC) // LANES:
                bm.append(blk.astype(BF16))
            else:
                cm.append(blk)
        xs_b, bm, cm = (jnp.concatenate(v, axis=1) for v in (xs_b, bm, cm))
        cbs = [[lax.dot_general(cm[rows, g * SSM_STATE:(g + 1) * SSM_STATE].astype(BF16),
                                bm[rows, g * SSM_STATE:(g + 1) * SSM_STATE], (((1,), (1,)), ((), ())),
                                preferred_element_type=F32) for g in range(SSM_GROUPS)]
               for rows in chunks]
        yn = [_ssd_scan(dec, cb, xs_b[rows], bm[rows], cm[rows], gate[rows], dexp_ref[...], gnw_ref[...],
                        state_ref, tick) for dec, cb, rows in zip(res["decays"], cbs, chunks)]
        return xraw[R - SUBLANES:R], jnp.concatenate(yn, axis=0)

    def out_proj(i, yn):
        out_ref[i * R:(i + 1) * R, :] = _dot(yn, pb_ref[...]).astype(BF16)

    n_blocks = tl // R
    tail = tail_ref[...]
    res = {"x": [xbuf_ref[...]], "z": [zbuf_ref[...]],
           "decays": [(a2buf_ref[c], eabuf_ref[c], r2buf_ref[c], wendbuf_ref[c], cdbuf_ref[c, 0:1, :])
                      for c in range(len(chunks))]}
    for i in range(n_blocks):
        if i + 1 < n_blocks:
            todo, nxt = projection(x_ref.at[(i + 1) * R:(i + 2) * R, :], False)
        else:
            todo, nxt = projection(xnext_ref, True)

        def tick():
            if todo:
                todo.pop(0)()

        for _ in range(SSD_FRONT_PIECES):
            tick()
        if i > 0:
            out_proj(i - 1, yn)
        tail, yn = process(i, res, tail, tick)
        while todo:
            tick()
        res = nxt
    out_proj(n_blocks - 1, yn)
    tail_ref[...] = tail


def _ssd_branch(x2, norm_w, w_z, w_xbc, w_dt, conv_w, conv_b, dt_bias, a_log, d_exp, gnorm_w, e2, w_pb,
                B, S, tl):
    nt = S // tl
    per_tile = tl // MXU_ROWS
    last_block = B * S // MXU_ROWS - 1
    return pl.pallas_call(
        functools.partial(_ssd_kernel, tl=tl),
        grid=(B, nt),
        in_specs=[
            pl.BlockSpec((tl, D_MODEL), lambda b, t: (b * nt + t, 0)),
            pl.BlockSpec((MXU_ROWS, D_MODEL), lambda b, t: (jnp.minimum((b * nt + t + 1) * per_tile, last_block), 0)),
            _resident((1, D_MODEL)),
            _resident((D_MODEL, SSM_INNER)),
            _resident((D_MODEL, SSM_XBC)),
            _resident((D_MODEL, LANES)),
            _resident((SSM_CONV, SSM_XBC)),
            _resident((1, SSM_XBC)),
            _resident((1, LANES)),
            _resident((1, LANES)),
            _resident((1, SSM_INNER)),
            _resident((1, SSM_INNER)),
            _resident((2 * LANES, SSM_INNER)),
            _resident((SSM_INNER, D_MODEL)),
        ],
        out_specs=pl.BlockSpec((tl, D_MODEL), lambda b, t: (b * nt + t, 0)),
        out_shape=jax.ShapeDtypeStruct((B * S, D_MODEL), BF16),
        scratch_shapes=[
            pltpu.VMEM((MXU_ROWS // SSD_Q, SSD_Q, LANES), F32),
            pltpu.VMEM((MXU_ROWS // SSD_Q, SSD_Q, LANES), F32),
            pltpu.VMEM((MXU_ROWS // SSD_Q, LANES, SSD_Q), F32),
            pltpu.VMEM((MXU_ROWS // SSD_Q, SSD_Q, SSM_INNER), F32),
            pltpu.VMEM((MXU_ROWS // SSD_Q, SUBLANES, SSM_INNER), F32),
            pltpu.VMEM((MXU_ROWS, SSM_XBC), F32),
            pltpu.VMEM((MXU_ROWS, SSM_INNER), F32),
            pltpu.VMEM((SUBLANES, SSM_XBC), F32),
            pltpu.VMEM((SSM_GROUPS, SSM_STATE, SSM_GROUP_WIDTH), F32),
        ],
        compiler_params=_params(2),
        name="ssd_branch",
    )(x2, x2, norm_w, w_z, w_xbc, w_dt, conv_w, conv_b, dt_bias, a_log, d_exp, gnorm_w, e2, w_pb)


def _ffn_kernel(h_ref, nw_ref, wup_ref, cw_ref, cb_ref, wdn_ref, fw_ref, out_ref, tail_ref, *, tm):
    W = FFN_COLS

    @pl.when(pl.program_id(1) == 0)
    def _():
        tail_ref[...] = jnp.zeros_like(tail_ref)

    h = h_ref[...]
    hn = _rmsnorm(h, nw_ref[...]).astype(BF16)
    tail = tail_ref[...]

    def up_proj(j):
        ups = []
        for base in (0, D_FF):
            cols = slice(base + j * W, base + (j + 1) * W)
            ups.append((cols, _dot(hn, wup_ref[:, cols])))
        return ups

    n_stage = D_FF // W
    acc_out = h
    ups = [up_proj(j) for j in range(FFN_LOOKAHEAD)]
    for j in range(n_stage):
        if j + FFN_LOOKAHEAD < n_stage:
            ups.append(up_proj(j + FFN_LOOKAHEAD))
        conv = []
        for cols, up in ups.pop(0):
            conv.append(_causal_conv(tail[:, cols], up, cw_ref[:, cols], cb_ref[:, cols]))
            tail_ref[:, cols] = up[tm - SUBLANES:tm]
        act = (_silu(conv[0]) * conv[1]).astype(BF16)
        acc_out = acc_out + _dot(act, wdn_ref[j * W:(j + 1) * W, :])
    out_ref[...] = _rmsnorm(acc_out, fw_ref[...])


def _ffn(h1, norm_w, w_up, conv_w, conv_b, w_dn, final_w, B, S, tm):
    nt = S // tm
    return pl.pallas_call(
        functools.partial(_ffn_kernel, tm=tm),
        grid=(B, nt),
        in_specs=[
            pl.BlockSpec((tm, D_MODEL), lambda b, t: (b * nt + t, 0)),
            _resident((1, D_MODEL)),
            _resident((D_MODEL, 2 * D_FF)),
            _resident((FFN_CONV, 2 * D_FF)),
            _resident((1, 2 * D_FF)),
            _resident((D_FF, D_MODEL)),
            _resident((1, D_MODEL)),
        ],
        out_specs=pl.BlockSpec((tm, D_MODEL), lambda b, t: (b * nt + t, 0)),
        out_shape=jax.ShapeDtypeStruct((B * S, D_MODEL), F32),
        scratch_shapes=[pltpu.VMEM((SUBLANES, 2 * D_FF), F32)],
        compiler_params=_params(2),
        name="ffn",
    )(h1, norm_w, w_up, conv_w, conv_b, w_dn, final_w)


def _pad_lanes(v):
    return jnp.pad(v.astype(F32), (0, LANES - v.shape[0])).reshape(1, LANES)


def _tile(n, want):
    t = min(n, want)
    assert n % t == 0, (n, want)
    return t


def kernel(x, mix_norm_w, w_in, gate_bias, gmlp_ln_w, gmlp_ln_b, gmlp_ws, gmlp_bs,
           ssm_conv_w, ssm_conv_b, ssm_dt_bias, ssm_a_log, ssm_d, ssm_norm_w,
           w_proj_a, w_proj_b, w_out, ffn_norm_w, ffn_w_up, ffn_conv_w, ffn_conv_b,
           ffn_w_down, final_norm_w):
    B, S, _ = x.shape
    T = B * S
    assert S % MXU_ROWS == 0
    assert w_in.shape[0] == 1, "single-layer trunk: the final norm is fused into the FFN kernel"
    l = 0
    head_of_col = jnp.arange(SSM_INNER)[None, :] // SSM_HEAD_DIM
    expand = (jnp.arange(LANES)[:, None] == head_of_col).astype(BF16)
    expand2 = jnp.concatenate([expand, expand], axis=0)

    x2 = x.reshape(T, D_MODEL)
    w_in_b = w_in[l].astype(BF16)
    mix_w = mix_norm_w[l].reshape(1, D_MODEL)
    w_dt = jnp.pad(w_in_b[:, COL_DT:], ((0, 0), (0, LANES - SSM_HEADS)))
    bs_full = jnp.repeat(gmlp_bs[l].T, GMLP_GDIM, axis=1)

    ybp = _ssd_branch(x2, mix_w, w_in_b[:, COL_Z:COL_XBC], w_in_b[:, COL_XBC:COL_DT], w_dt,
                      ssm_conv_w[l], ssm_conv_b[l].reshape(1, SSM_XBC),
                      _pad_lanes(ssm_dt_bias[l]), _pad_lanes(ssm_a_log[l]),
                      jnp.repeat(ssm_d[l].astype(F32), SSM_HEAD_DIM).reshape(1, SSM_INNER),
                      ssm_norm_w[l].reshape(1, SSM_INNER), expand2, w_proj_b[l].astype(BF16),
                      B, S, _tile(S, 512))
    h1 = _mix(x2, ybp, mix_w, w_in_b[:, COL_ZA:COL_Z], gmlp_ln_w[l].reshape(1, GMLP_WIDTH),
              gmlp_ln_b[l].reshape(1, GMLP_WIDTH), gmlp_ws[l], bs_full, w_in_b[:, COL_GATES:COL_ZA],
              gate_bias[l].reshape(1, 2 * D_MODEL), w_proj_a[l].astype(BF16), w_out[l].astype(BF16),
              _tile(T, 1024))
    out = _ffn(h1, ffn_norm_w[l].reshape(1, D_MODEL), ffn_w_up[l].astype(BF16), ffn_conv_w[l],
               ffn_conv_b[l].reshape(1, 2 * D_FF), ffn_w_down[l].astype(BF16),
               final_norm_w.reshape(1, D_MODEL), B, S, _tile(S, 256))
    return out.reshape(B, S, D_MODEL)
```
